```python
import math
import jax, jax.numpy as jnp
from jax import lax
import numpy as np

D_MODEL = 2048
BATCH = 1
SEQ = 8192
DEPTH = 1
DEC_BATCH = 32
DEC_SEQ = 1
PAST_LEN = 8192
PAGE_SIZE = 128

H_NSA = 16
G_NSA = 2
HPG_NSA = H_NSA // G_NSA
DH_NSA = 64
L_CMP = 32
STRIDE_CMP = 16
CMP_HID = 2 * DH_NSA
L_SLC = 64
N_SEL = 16
WINDOW = 512
H_DIFF = 8
DH_DIFF = 64
DV_DIFF = 2 * DH_DIFF
N_KEYS = 128
N_EXPERTS = N_KEYS * N_KEYS
PEER_HEADS = 8
PEER_DQ = 256
PEER_TOPK = 16
ROPE_THETA = 10000.0
Q_BLOCK = 128
LN_EPS = 1e-5
NEG_INF = -1e30
SEL_BONUS = 1e6
ALPHA = (2.0 * DEPTH) ** 0.25
BETA = (8.0 * DEPTH) ** -0.25

W_NSA_Q = H_NSA * DH_NSA
W_NSA_KV = G_NSA * DH_NSA
W_NSA_GATE = H_NSA * 3
W_DIFF_QK = H_DIFF * 2 * DH_DIFF
W_DIFF_V = H_DIFF * DV_DIFF
COL_SIZES = (W_NSA_Q, W_NSA_KV, W_NSA_KV, W_NSA_KV, W_NSA_KV, W_NSA_KV, W_NSA_KV, W_NSA_GATE,
             W_DIFF_QK, W_DIFF_QK, W_DIFF_V, 2 * D_MODEL)
COL_IS_VALUE = (False, False, True, False, True, False, True, False, False, False, True, False)
D_IN = sum(COL_SIZES)

kernel_name = 'nsa_diffattn_peer_hybrid_step'


def layer_norm(x, g, b):
    xf = x.astype(jnp.float32)
    mu = xf.mean(-1, keepdims=True)
    var = jnp.square(xf - mu).mean(-1, keepdims=True)
    y = (xf - mu) * lax.rsqrt(var + LN_EPS) * g.astype(jnp.float32) + b.astype(jnp.float32)
    return y.astype(x.dtype)


def rope(x, pos):
    d = x.shape[-1]
    half = d // 2
    inv = ROPE_THETA ** (-jnp.arange(half, dtype=jnp.float32) / half)
    ang = pos.astype(jnp.float32)[:, None] * inv[None, :]
    cos, sin = jnp.cos(ang)[None, :, None, :], jnp.sin(ang)[None, :, None, :]
    xf = x.astype(jnp.float32)
    x1, x2 = xf[..., :half], xf[..., half:]
    return jnp.concatenate([x1 * cos - x2 * sin, x1 * sin + x2 * cos], -1).astype(x.dtype)


def masked_softmax(s, mask):
    s = jnp.where(mask, s.astype(jnp.float32), NEG_INF)
    return jnp.where(mask, jax.nn.softmax(s, axis=-1), 0.0)


def split_cols(z):
    out, start = [], 0
    for size in COL_SIZES:
        out.append(z[..., start:start + size])
        start += size
    return out


def gather_pages(cache, layer, page_table):
    b, n_pages = page_table.shape
    rows = cache[layer, page_table]
    return rows.reshape((b, n_pages * PAGE_SIZE) + cache.shape[3:])


def mixer_project(x, pos, w_in):
    b, t, _ = x.shape
    q_n, kc, vc, ks, vs, kw, vw, g_n, q_d, k_d, v_d, g_m = split_cols(x @ w_in)
    kvh = lambda a: a.reshape(b, t, G_NSA, DH_NSA)
    q_n = rope(q_n.reshape(b, t, H_NSA, DH_NSA), pos).reshape(b, t, G_NSA, HPG_NSA, DH_NSA)
    kc, ks, kw = rope(kvh(kc), pos), rope(kvh(ks), pos), rope(kvh(kw), pos)
    vc, vs, vw = kvh(vc), kvh(vs), kvh(vw)
    g_n = jax.nn.sigmoid(g_n.reshape(b, t, H_NSA, 3))
    q_d = q_d.reshape(b, t, H_DIFF, 2, DH_DIFF)
    k_d = k_d.reshape(b, t, H_DIFF, 2, DH_DIFF)
    q1, q2 = rope(q_d[:, :, :, 0], pos), rope(q_d[:, :, :, 1], pos)
    k_d = jnp.concatenate([rope(k_d[:, :, :, 0], pos), rope(k_d[:, :, :, 1], pos)], -1)
    v_d = v_d.reshape(b, t, H_DIFF, DV_DIFF)
    g_m = jax.nn.sigmoid(g_m.reshape(b, t, 2, D_MODEL))
    return (q_n, kc, vc, ks, vs, kw, vw, g_n, q1, q2, k_d, v_d, g_m)


def compress(rows, w1, pe, w2):
    b, t = rows.shape[:2]
    nc = (t - L_CMP) // STRIDE_CMP + 1
    r = rows[:, :STRIDE_CMP * (nc + 1)].reshape(b, nc + 1, STRIDE_CMP, G_NSA, DH_NSA)
    first = jnp.einsum('bnlgd,lde->bnge', r, w1[:STRIDE_CMP])
    second = jnp.einsum('bnlgd,lde->bnge', r, w1[STRIDE_CMP:])
    pe_term = jnp.einsum('ld,lde->e', pe, w1)
    hid = jax.nn.gelu(first[:, :-1] + second[:, 1:] + pe_term, approximate=False)
    return jnp.einsum('bnge,ed->bngd', hid, w2)


def nsa_attend(q, qpos, kc, vc, sel_fn, kw, vw, wpos, gates, n_slc):
    b, tq = q.shape[:2]
    nc = kc.shape[1]
    scale = DH_NSA ** -0.5
    t = qpos[:, None]
    cend = jnp.arange(nc, dtype=jnp.int32) * STRIDE_CMP + (L_CMP - 1)
    s = jnp.einsum('btghd,bngd->btghn', q, kc).astype(jnp.float32) * scale
    p_cmp = masked_softmax(s, (cend[None, :] <= t)[None, :, None, None, :])
    o_cmp = jnp.einsum('btghn,bngd->btghd', p_cmp.astype(vc.dtype), vc)
    ratio = L_SLC // STRIDE_CMP
    imp = jnp.pad(p_cmp.sum(axis=3), ((0, 0), (0, 0), (0, 0), (0, n_slc * ratio - nc)))
    imp = imp.reshape(b, tq, G_NSA, n_slc, ratio).sum(-1)
    blk = jnp.arange(n_slc, dtype=jnp.int32)[None, :]
    cur = (qpos // L_SLC)[:, None]
    forced = (blk == 0) | (blk == cur) | (blk == cur - 1)
    future = blk * L_SLC > t
    score = jnp.where(future[None, :, None, :], -SEL_BONUS,
                      imp + jnp.where(forced, SEL_BONUS, 0.0)[None, :, None, :])
    vals, idx = lax.top_k(score, min(N_SEL, n_slc))
    valid = vals > -0.5 * SEL_BONUS
    spos = idx[..., None] * L_SLC + jnp.arange(L_SLC, dtype=jnp.int32)
    ks, vs = sel_fn(spos)
    smask = (valid[..., None] & (spos <= qpos[None, :, None, None, None])).reshape(b, tq, G_NSA, -1)
    ks = ks.reshape(b, tq, G_NSA, -1, DH_NSA)
    vs = vs.reshape(b, tq, G_NSA, -1, DH_NSA)
    s = jnp.einsum('btghd,btgkd->btghk', q, ks).astype(jnp.float32) * scale
    p = masked_softmax(s, smask[:, :, :, None, :])
    o_slc = jnp.einsum('btghk,btgkd->btghd', p.astype(vs.dtype), vs)
    dist = t - wpos[None, :]
    wmask = (dist >= 0) & (dist <= WINDOW) & (wpos[None, :] >= 0)
    s = jnp.einsum('btghd,bsgd->btghs', q, kw).astype(jnp.float32) * scale
    p = masked_softmax(s, wmask[None, :, None, None, :])
    o_win = jnp.einsum('btghs,bsgd->btghd', p.astype(vw.dtype), vw)
    g = gates.reshape(b, tq, G_NSA, HPG_NSA, 3)
    o = g[..., 0:1] * o_cmp + g[..., 1:2] * o_slc + g[..., 2:3] * o_win
    return o.reshape(b, tq, H_NSA * DH_NSA)


def nsa_prompt(q, kc_rows, vc_rows, ks_rows, vs_rows, kw_rows, vw_rows, gates, cmp_k, cmp_v):
    b, t = q.shape[:2]
    kc, vc = compress(kc_rows, *cmp_k), compress(vc_rows, *cmp_v)
    n_slc = -(-t // L_SLC)
    bidx = jnp.arange(b)[:, None, None, None, None]
    gidx = jnp.arange(G_NSA)[None, None, :, None, None]

    def sel_fn(spos):
        p = jnp.clip(spos, 0, t - 1)
        return ks_rows[bidx, p, gidx], vs_rows[bidx, p, gidx]

    pad = ((0, 0), (WINDOW, 0), (0, 0), (0, 0))
    kw_pad, vw_pad = jnp.pad(kw_rows, pad), jnp.pad(vw_rows, pad)
    n_blk = t // Q_BLOCK
    q_blk = q.reshape(b, n_blk, Q_BLOCK, G_NSA, HPG_NSA, DH_NSA).swapaxes(0, 1)
    g_blk = gates.reshape(b, n_blk, Q_BLOCK, H_NSA, 3).swapaxes(0, 1)

    def one_block(args):
        i, qb, gb = args
        q0 = i * Q_BLOCK
        qpos = q0 + jnp.arange(Q_BLOCK, dtype=jnp.int32)
        wpos = q0 - WINDOW + jnp.arange(WINDOW + Q_BLOCK, dtype=jnp.int32)
        kw = lax.dynamic_slice_in_dim(kw_pad, q0, WINDOW + Q_BLOCK, axis=1)
        vw = lax.dynamic_slice_in_dim(vw_pad, q0, WINDOW + Q_BLOCK, axis=1)
        return nsa_attend(qb, qpos, kc, vc, sel_fn, kw, vw, wpos, gb, n_slc)

    out = lax.map(one_block, (jnp.arange(n_blk, dtype=jnp.int32), q_blk, g_blk))
    return out.swapaxes(0, 1).reshape(b, t, H_NSA * DH_NSA)


def nsa_sample(q, kc_new, vc_new, ks_new, vs_new, kw_new, vw_new, gates, cache_cmp_k, cache_cmp_v,
               cache_slc_k, cache_slc_v, win_k, win_v, page_table, layer, cmp_k, cmp_v):
    b, tn = q.shape[:2]
    past_len = page_table.shape[1] * PAGE_SIZE
    t_total = past_len + tn
    kc = compress(jnp.concatenate([gather_pages(cache_cmp_k, layer, page_table), kc_new], 1), *cmp_k)
    vc = compress(jnp.concatenate([gather_pages(cache_cmp_v, layer, page_table), vc_new], 1), *cmp_v)
    bidx = jnp.arange(b)[:, None, None, None, None]
    gidx = jnp.arange(G_NSA)[None, None, :, None, None]

    def sel_fn(spos):
        in_past = (spos < past_len)[..., None]
        pp = jnp.clip(spos, 0, past_len - 1)
        page = page_table[bidx, pp // PAGE_SIZE]
        off = pp % PAGE_SIZE
        pn = jnp.clip(spos - past_len, 0, tn - 1)
        k = jnp.where(in_past, cache_slc_k[layer, page, off, gidx], ks_new[bidx, pn, gidx])
        v = jnp.where(in_past, cache_slc_v[layer, page, off, gidx], vs_new[bidx, pn, gidx])
        return k, v

    wbuf = win_k.shape[1]
    kw = jnp.concatenate([win_k, kw_new], 1)
    vw = jnp.concatenate([win_v, vw_new], 1)
    wpos = past_len - wbuf + jnp.arange(wbuf + tn, dtype=jnp.int32)
    qpos = past_len + jnp.arange(tn, dtype=jnp.int32)
    o = nsa_attend(q, qpos, kc, vc, sel_fn, kw, vw, wpos, gates, -(-t_total // L_SLC))
    keep = min(WINDOW, t_total)
    return o, kw[:, wbuf + tn - keep:], vw[:, wbuf + tn - keep:]


def diff_attend(q1, q2, k, v, qpos, kpos, lam, subln_g, lam_init):
    b, tq = q1.shape[:2]
    scale = DH_DIFF ** -0.5
    mask = (kpos[None, :] <= qpos[:, None])[None, None]
    k1, k2 = k[..., :DH_DIFF], k[..., DH_DIFF:]
    a1 = masked_softmax(jnp.einsum('bthd,bshd->bhts', q1, k1).astype(jnp.float32) * scale, mask)
    a2 = masked_softmax(jnp.einsum('bthd,bshd->bhts', q2, k2).astype(jnp.float32) * scale, mask)
    o = jnp.einsum('bhts,bshd->bthd', (a1 - lam * a2).astype(v.dtype), v).astype(jnp.float32)
    o = o * lax.rsqrt(jnp.mean(jnp.square(o), -1, keepdims=True) + LN_EPS) * subln_g.astype(jnp.float32)
    return (o * (1.0 - lam_init)).astype(v.dtype).reshape(b, tq, H_DIFF * DV_DIFF)


def diff_prompt(q1, q2, k, v, lam, subln_g, lam_init):
    b, t = q1.shape[:2]
    n_blk = t // Q_BLOCK
    kpos = jnp.arange(t, dtype=jnp.int32)
    qb1 = q1.reshape(b, n_blk, Q_BLOCK, H_DIFF, DH_DIFF).swapaxes(0, 1)
    qb2 = q2.reshape(b, n_blk, Q_BLOCK, H_DIFF, DH_DIFF).swapaxes(0, 1)

    def one_block(args):
        i, a, c = args
        qpos = i * Q_BLOCK + jnp.arange(Q_BLOCK, dtype=jnp.int32)
        return diff_attend(a, c, k, v, qpos, kpos, lam, subln_g, lam_init)

    out = lax.map(one_block, (jnp.arange(n_blk, dtype=jnp.int32), qb1, qb2))
    return out.swapaxes(0, 1).reshape(b, t, H_DIFF * DV_DIFF)


def mixer_output(o_nsa, o_diff, g_m, w_br_nsa, w_br_diff, w_out):
    m = g_m[:, :, 0] * (o_nsa @ w_br_nsa) + g_m[:, :, 1] * (o_diff @ w_br_diff)
    return m @ w_out


def peer_ffn(xf, wq, subkeys, u, v):
    n = xf.shape[0]
    q = (xf @ wq).reshape(n, PEER_HEADS, 2, PEER_DQ // 2)
    s = jnp.einsum('nhpc,hpkc->nhpk', q, subkeys).astype(jnp.float32)
    s1, i1 = lax.top_k(s[:, :, 0], PEER_TOPK)
    s2, i2 = lax.top_k(s[:, :, 1], PEER_TOPK)
    cand = (s1[..., :, None] + s2[..., None, :]).reshape(n, PEER_HEADS, PEER_TOPK * PEER_TOPK)
    cid = (i1[..., :, None] * N_KEYS + i2[..., None, :]).reshape(n, PEER_HEADS, PEER_TOPK * PEER_TOPK)
    top, sel = lax.top_k(cand, PEER_TOPK)
    eid = jnp.take_along_axis(cid, sel, axis=-1)
    g = jax.nn.softmax(top, axis=-1)
    act = jax.nn.gelu(jnp.einsum('nd,nhkd->nhk', xf, u[eid]), approximate=False)
    return jnp.einsum('nhk,nhkd->nd', (g * act.astype(jnp.float32)).astype(xf.dtype), v[eid])


def post_block(x, mix, ln1_g, ln1_b, ln2_g, ln2_b, peer_w):
    h = layer_norm(ALPHA * x + mix, ln1_g, ln1_b)
    b, t, d = h.shape
    n = b * t
    n_blk = -(-n // Q_BLOCK)
    hf = jnp.pad(h.reshape(n, d), ((0, n_blk * Q_BLOCK - n), (0, 0)))
    f = lax.map(lambda blk: peer_ffn(blk, *peer_w), hf.reshape(n_blk, Q_BLOCK, d))
    f = f.reshape(n_blk * Q_BLOCK, d)[:n].reshape(b, t, d)
    return layer_norm(ALPHA * h + f, ln2_g, ln2_b)


def setup_inputs(seed: int = 0) -> dict:
    key = jax.random.key(seed)
    keys = iter(jax.random.split(key, 48))

    def nrm(shape, scale):
        return jax.random.normal(next(keys), shape, jnp.float32) * scale

    n_pages = PAST_LEN // PAGE_SIZE
    n_used = DEC_BATCH * n_pages
    n_pool = n_used + max(1, n_used // 4)
    page_table = jax.random.permutation(next(keys), n_pool)[:n_used].reshape(DEC_BATCH, n_pages).astype(jnp.int32)
    win_buf = min(WINDOW, PAST_LEN)
    col_scale = np.ones((D_IN,), np.float32)
    start = 0
    for size, is_value in zip(COL_SIZES, COL_IS_VALUE):
        if is_value:
            col_scale[start:start + size] = BETA
        start += size
    w_nsa_out = H_NSA * DH_NSA
    w_diff_out = H_DIFF * DV_DIFF
    return {
        'x_prompt': nrm((BATCH, SEQ, D_MODEL), 1.0),
        'x_sample': nrm((DEC_BATCH, DEC_SEQ, D_MODEL), 1.0),
        'cache_diff_k': nrm((DEPTH, n_pool, PAGE_SIZE, H_DIFF, 2 * DH_DIFF), 1.0),
        'cache_diff_v': nrm((DEPTH, n_pool, PAGE_SIZE, H_DIFF, DV_DIFF), 1.0),
        'cache_nsa_cmp_k': nrm((DEPTH, n_pool, PAGE_SIZE, G_NSA, DH_NSA), 1.0),
        'cache_nsa_cmp_v': nrm((DEPTH, n_pool, PAGE_SIZE, G_NSA, DH_NSA), 1.0),
        'cache_nsa_slc_k': nrm((DEPTH, n_pool, PAGE_SIZE, G_NSA, DH_NSA), 1.0),
        'cache_nsa_slc_v': nrm((DEPTH, n_pool, PAGE_SIZE, G_NSA, DH_NSA), 1.0),
        'state_nsa_win_k': nrm((DEPTH, DEC_BATCH, win_buf, G_NSA, DH_NSA), 1.0),
        'state_nsa_win_v': nrm((DEPTH, DEC_BATCH, win_buf, G_NSA, DH_NSA), 1.0),
        'page_table': page_table,
        'w_in': nrm((DEPTH, D_MODEL, D_IN), D_MODEL ** -0.5) * jnp.asarray(col_scale),
        'cmp_w1_k': nrm((DEPTH, L_CMP, DH_NSA, CMP_HID), (L_CMP * DH_NSA) ** -0.5),
        'cmp_pe_k': nrm((DEPTH, L_CMP, DH_NSA), 0.1),
        'cmp_w2_k': nrm((DEPTH, CMP_HID, DH_NSA), CMP_HID ** -0.5),
        'cmp_w1_v': nrm((DEPTH, L_CMP, DH_NSA, CMP_HID), (L_CMP * DH_NSA) ** -0.5),
        'cmp_pe_v': nrm((DEPTH, L_CMP, DH_NSA), 0.1),
        'cmp_w2_v': nrm((DEPTH, CMP_HID, DH_NSA), CMP_HID ** -0.5),
        'lambda_q1': nrm((DEPTH, DH_DIFF), 0.1),
        'lambda_k1': nrm((DEPTH, DH_DIFF), 0.1),
        'lambda_q2': nrm((DEPTH, DH_DIFF), 0.1),
        'lambda_k2': nrm((DEPTH, DH_DIFF), 0.1),
        'diff_subln_g': 1.0 + nrm((DEPTH, DV_DIFF), 0.02),
        'w_br_nsa': nrm((DEPTH, w_nsa_out, D_MODEL), BETA * w_nsa_out ** -0.5),
        'w_br_diff': nrm((DEPTH, w_diff_out, D_MODEL), BETA * w_diff_out ** -0.5),
        'w_out': nrm((DEPTH, D_MODEL, D_MODEL), BETA * D_MODEL ** -0.5),
        'ln1_g': 1.0 + nrm((DEPTH, D_MODEL), 0.02),
        'ln1_b': nrm((DEPTH, D_MODEL), 0.02),
        'peer_wq': nrm((DEPTH, D_MODEL, PEER_HEADS * PEER_DQ), D_MODEL ** -0.5),
        'peer_subkeys': nrm((DEPTH, PEER_HEADS, 2, N_KEYS, PEER_DQ // 2), (PEER_DQ // 2) ** -0.5),
        'peer_u': nrm((DEPTH, N_EXPERTS, D_MODEL), D_MODEL ** -0.5),
        'peer_v': nrm((DEPTH, N_EXPERTS, D_MODEL), BETA),
        'ln2_g': 1.0 + nrm((DEPTH, D_MODEL), 0.02),
        'ln2_b': nrm((DEPTH, D_MODEL), 0.02),
    }


def reference(x_prompt, x_sample, cache_diff_k, cache_diff_v, cache_nsa_cmp_k, cache_nsa_cmp_v,
              cache_nsa_slc_k, cache_nsa_slc_v, state_nsa_win_k, state_nsa_win_v, page_table,
              w_in, cmp_w1_k, cmp_pe_k, cmp_w2_k, cmp_w1_v, cmp_pe_v, cmp_w2_v,
              lambda_q1, lambda_k1, lambda_q2, lambda_k2, diff_subln_g,
              w_br_nsa, w_br_diff, w_out, ln1_g, ln1_b,
              peer_wq, peer_subkeys, peer_u, peer_v, ln2_g, ln2_b):
    past_len = page_table.shape[1] * PAGE_SIZE
    seq, dec_seq = x_prompt.shape[1], x_sample.shape[1]
    pos_p = jnp.arange(seq, dtype=jnp.int32)
    pos_s = past_len + jnp.arange(dec_seq, dtype=jnp.int32)
    xp, xs = x_prompt, x_sample
    sp = [[] for _ in range(8)]
    ss = [[] for _ in range(8)]
    for l in range(DEPTH):
        lam_init = 0.8 - 0.6 * math.exp(-0.3 * l)
        lam = (jnp.exp(jnp.sum(lambda_q1[l].astype(jnp.float32) * lambda_k1[l].astype(jnp.float32)))
               - jnp.exp(jnp.sum(lambda_q2[l].astype(jnp.float32) * lambda_k2[l].astype(jnp.float32)))
               + lam_init)
        cmp_k = (cmp_w1_k[l], cmp_pe_k[l], cmp_w2_k[l])
        cmp_v = (cmp_w1_v[l], cmp_pe_v[l], cmp_w2_v[l])
        peer_w = (peer_wq[l], peer_subkeys[l], peer_u[l], peer_v[l])

        q_n, kc, vc, ks, vs, kw, vw, g_n, q1, q2, kd, vd, g_m = mixer_project(xp, pos_p, w_in[l])
        o_n = nsa_prompt(q_n, kc, vc, ks, vs, kw, vw, g_n, cmp_k, cmp_v)
        o_d = diff_prompt(q1, q2, kd, vd, lam, diff_subln_g[l], lam_init)
        mix = mixer_output(o_n, o_d, g_m, w_br_nsa[l], w_br_diff[l], w_out[l])
        keep = min(WINDOW, seq)
        for lst, a in zip(sp, (kd, vd, kc, vc, ks, vs, kw[:, seq - keep:], vw[:, seq - keep:])):
            lst.append(a)
        xp = post_block(xp, mix, ln1_g[l], ln1_b[l], ln2_g[l], ln2_b[l], peer_w)

        q_n, kc, vc, ks, vs, kw, vw, g_n, q1, q2, kd, vd, g_m = mixer_project(xs, pos_s, w_in[l])
        o_n, win_k_new, win_v_new = nsa_sample(q_n, kc, vc, ks, vs, kw, vw, g_n,
                                               cache_nsa_cmp_k, cache_nsa_cmp_v, cache_nsa_slc_k, cache_nsa_slc_v,
                                               state_nsa_win_k[l], state_nsa_win_v[l], page_table, l, cmp_k, cmp_v)
        k_all = jnp.concatenate([gather_pages(cache_diff_k, l, page_table), kd], 1)
        v_all = jnp.concatenate([gather_pages(cache_diff_v, l, page_table), vd], 1)
        kpos = jnp.arange(past_len + dec_seq, dtype=jnp.int32)
        o_d = diff_attend(q1, q2, k_all, v_all, pos_s, kpos, lam, diff_subln_g[l], lam_init)
        mix = mixer_output(o_n, o_d, g_m, w_br_nsa[l], w_br_diff[l], w_out[l])
        for lst, a in zip(ss, (kd, vd, kc, vc, ks, vs, win_k_new, win_v_new)):
            lst.append(a)
        xs = post_block(xs, mix, ln1_g[l], ln1_b[l], ln2_g[l], ln2_b[l], peer_w)

    p_diff_k, p_diff_v, p_cmp_k, p_cmp_v, p_slc_k, p_slc_v, p_win_k, p_win_v = [jnp.stack(a, 0) for a in sp]
    s_diff_k, s_diff_v, s_cmp_k, s_cmp_v, s_slc_k, s_slc_v, s_win_k, s_win_v = [jnp.stack(a, 0) for a in ss]
    return (xp, xs, p_diff_k, p_diff_v, p_cmp_k, p_cmp_v, p_slc_k, p_slc_v, p_win_k, p_win_v,
            s_diff_k, s_diff_v, s_cmp_k, s_cmp_v, s_slc_k, s_slc_v, s_win_k, s_win_v)
```

```python
import functools
import math

import jax
import jax.numpy as jnp
from jax import lax
import numpy as np
from jax.experimental import pallas as pl
from jax.experimental.pallas import tpu as pltpu

D_MODEL = 2048
DEPTH = 1
PAGE_SIZE = 128
H_NSA = 16
G_NSA = 2
HPG_NSA = H_NSA // G_NSA
DH_NSA = 64
L_CMP = 32
STRIDE_CMP = 16
CMP_HID = 2 * DH_NSA
L_SLC = 64
N_SEL = 16
WINDOW = 512
H_DIFF = 8
DH_DIFF = 64
DV_DIFF = 2 * DH_DIFF
N_KEYS = 128
N_EXPERTS = N_KEYS * N_KEYS
PEER_HEADS = 8
PEER_DQ = 256
PEER_TOPK = 16
ROPE_THETA = 10000.0
Q_BLOCK = 128
LN_EPS = 1e-5
NEG_INF = -1e30
SEL_BONUS = 1e6
ALPHA = (2.0 * DEPTH) ** 0.25

W_NSA_Q = H_NSA * DH_NSA
W_NSA_KV = G_NSA * DH_NSA
W_NSA_GATE = H_NSA * 3
W_DIFF_QK = H_DIFF * 2 * DH_DIFF
W_DIFF_V = H_DIFF * DV_DIFF
COL_SIZES = (W_NSA_Q, W_NSA_KV, W_NSA_KV, W_NSA_KV, W_NSA_KV, W_NSA_KV, W_NSA_KV, W_NSA_GATE,
             W_DIFF_QK, W_DIFF_QK, W_DIFF_V, 2 * D_MODEL)


def layer_norm(x, g, b):
    xf = x.astype(jnp.float32)
    mu = xf.mean(-1, keepdims=True)
    var = jnp.square(xf - mu).mean(-1, keepdims=True)
    y = (xf - mu) * lax.rsqrt(var + LN_EPS) * g.astype(jnp.float32) + b.astype(jnp.float32)
    return y.astype(x.dtype)


def rope(x, pos):
    d = x.shape[-1]
    half = d // 2
    inv = ROPE_THETA ** (-jnp.arange(half, dtype=jnp.float32) / half)
    ang = pos.astype(jnp.float32)[:, None] * inv[None, :]
    cos, sin = jnp.cos(ang)[None, :, None, :], jnp.sin(ang)[None, :, None, :]
    xf = x.astype(jnp.float32)
    x1, x2 = xf[..., :half], xf[..., half:]
    return jnp.concatenate([x1 * cos - x2 * sin, x1 * sin + x2 * cos], -1).astype(x.dtype)


def masked_softmax(s, mask):
    s = jnp.where(mask, s.astype(jnp.float32), NEG_INF)
    return jnp.where(mask, jax.nn.softmax(s, axis=-1), 0.0)


def split_cols(z):
    out, start = [], 0
    for size in COL_SIZES:
        out.append(z[..., start:start + size])
        start += size
    return out


def gather_pages(cache, layer, page_table):
    b, n_pages = page_table.shape
    rows = cache[layer, page_table]
    return rows.reshape((b, n_pages * PAGE_SIZE) + cache.shape[3:])


def mixer_project(x, pos, w_in):
    b, t, _ = x.shape
    q_n, kc, vc, ks, vs, kw, vw, g_n, q_d, k_d, v_d, g_m = split_cols(x @ w_in)
    kvh = lambda a: a.reshape(b, t, G_NSA, DH_NSA)
    q_n = rope(q_n.reshape(b, t, H_NSA, DH_NSA), pos).reshape(b, t, G_NSA, HPG_NSA, DH_NSA)
    kc, ks, kw = rope(kvh(kc), pos), rope(kvh(ks), pos), rope(kvh(kw), pos)
    vc, vs, vw = kvh(vc), kvh(vs), kvh(vw)
    g_n = jax.nn.sigmoid(g_n.reshape(b, t, H_NSA, 3))
    q_d = q_d.reshape(b, t, H_DIFF, 2, DH_DIFF)
    k_d = k_d.reshape(b, t, H_DIFF, 2, DH_DIFF)
    q1, q2 = rope(q_d[:, :, :, 0], pos), rope(q_d[:, :, :, 1], pos)
    k_d = jnp.concatenate([rope(k_d[:, :, :, 0], pos), rope(k_d[:, :, :, 1], pos)], -1)
    v_d = v_d.reshape(b, t, H_DIFF, DV_DIFF)
    g_m = jax.nn.sigmoid(g_m.reshape(b, t, 2, D_MODEL))
    return (q_n, kc, vc, ks, vs, kw, vw, g_n, q1, q2, k_d, v_d, g_m)


def compress(rows, w1, pe, w2):
    b, t = rows.shape[:2]
    nc = (t - L_CMP) // STRIDE_CMP + 1
    r = rows[:, :STRIDE_CMP * (nc + 1)].reshape(b, nc + 1, STRIDE_CMP, G_NSA, DH_NSA)
    first = jnp.einsum('bnlgd,lde->bnge', r, w1[:STRIDE_CMP])
    second = jnp.einsum('bnlgd,lde->bnge', r, w1[STRIDE_CMP:])
    pe_term = jnp.einsum('ld,lde->e', pe, w1)
    hid = jax.nn.gelu(first[:, :-1] + second[:, 1:] + pe_term, approximate=False)
    return jnp.einsum('bnge,ed->bngd', hid, w2)


def nsa_attend(q, qpos, kc, vc, sel_fn, kw, vw, wpos, gates, n_slc):
    b, tq = q.shape[:2]
    nc = kc.shape[1]
    scale = DH_NSA ** -0.5
    t = qpos[:, None]
    cend = jnp.arange(nc, dtype=jnp.int32) * STRIDE_CMP + (L_CMP - 1)
    s = jnp.einsum('btghd,bngd->btghn', q, kc).astype(jnp.float32) * scale
    p_cmp = masked_softmax(s, (cend[None, :] <= t)[None, :, None, None, :])
    o_cmp = jnp.einsum('btghn,bngd->btghd', p_cmp.astype(vc.dtype), vc)
    ratio = L_SLC // STRIDE_CMP
    imp = jnp.pad(p_cmp.sum(axis=3), ((0, 0), (0, 0), (0, 0), (0, n_slc * ratio - nc)))
    imp = imp.reshape(b, tq, G_NSA, n_slc, ratio).sum(-1)
    blk = jnp.arange(n_slc, dtype=jnp.int32)[None, :]
    cur = (qpos // L_SLC)[:, None]
    forced = (blk == 0) | (blk == cur) | (blk == cur - 1)
    future = blk * L_SLC > t
    score = jnp.where(future[None, :, None, :], -SEL_BONUS,
                      imp + jnp.where(forced, SEL_BONUS, 0.0)[None, :, None, :])
    vals, idx = lax.top_k(score, min(N_SEL, n_slc))
    valid = vals > -0.5 * SEL_BONUS
    spos = idx[..., None] * L_SLC + jnp.arange(L_SLC, dtype=jnp.int32)
    ks, vs = sel_fn(spos)
    smask = (valid[..., None] & (spos <= qpos[None, :, None, None, None])).reshape(b, tq, G_NSA, -1)
    ks = ks.reshape(b, tq, G_NSA, -1, DH_NSA)
    vs = vs.reshape(b, tq, G_NSA, -1, DH_NSA)
    s = jnp.einsum('btghd,btgkd->btghk', q, ks).astype(jnp.float32) * scale
    p = masked_softmax(s, smask[:, :, :, None, :])
    o_slc = jnp.einsum('btghk,btgkd->btghd', p.astype(vs.dtype), vs)
    dist = t - wpos[None, :]
    wmask = (dist >= 0) & (dist <= WINDOW) & (wpos[None, :] >= 0)
    s = jnp.einsum('btghd,bsgd->btghs', q, kw).astype(jnp.float32) * scale
    p = masked_softmax(s, wmask[None, :, None, None, :])
    o_win = jnp.einsum('btghs,bsgd->btghd', p.astype(vw.dtype), vw)
    g = gates.reshape(b, tq, G_NSA, HPG_NSA, 3)
    o = g[..., 0:1] * o_cmp + g[..., 1:2] * o_slc + g[..., 2:3] * o_win
    return o.reshape(b, tq, H_NSA * DH_NSA)


def nsa_prompt(q, kc_rows, vc_rows, ks_rows, vs_rows, kw_rows, vw_rows, gates, cmp_k, cmp_v):
    b, t = q.shape[:2]
    kc, vc = compress(kc_rows, *cmp_k), compress(vc_rows, *cmp_v)
    n_slc = -(-t // L_SLC)
    bidx = jnp.arange(b)[:, None, None, None, None]
    gidx = jnp.arange(G_NSA)[None, None, :, None, None]

    def sel_fn(spos):
        p = jnp.clip(spos, 0, t - 1)
        return ks_rows[bidx, p, gidx], vs_rows[bidx, p, gidx]

    pad = ((0, 0), (WINDOW, 0), (0, 0), (0, 0))
    kw_pad, vw_pad = jnp.pad(kw_rows, pad), jnp.pad(vw_rows, pad)
    n_blk = t // Q_BLOCK
    q_blk = q.reshape(b, n_blk, Q_BLOCK, G_NSA, HPG_NSA, DH_NSA).swapaxes(0, 1)
    g_blk = gates.reshape(b, n_blk, Q_BLOCK, H_NSA, 3).swapaxes(0, 1)

    def one_block(args):
        i, qb, gb = args
        q0 = i * Q_BLOCK
        qpos = q0 + jnp.arange(Q_BLOCK, dtype=jnp.int32)
        wpos = q0 - WINDOW + jnp.arange(WINDOW + Q_BLOCK, dtype=jnp.int32)
        kw = lax.dynamic_slice_in_dim(kw_pad, q0, WINDOW + Q_BLOCK, axis=1)
        vw = lax.dynamic_slice_in_dim(vw_pad, q0, WINDOW + Q_BLOCK, axis=1)
        return nsa_attend(qb, qpos, kc, vc, sel_fn, kw, vw, wpos, gb, n_slc)

    out = lax.map(one_block, (jnp.arange(n_blk, dtype=jnp.int32), q_blk, g_blk))
    return out.swapaxes(0, 1).reshape(b, t, H_NSA * DH_NSA)


def nsa_sample(q, kc_new, vc_new, ks_new, vs_new, kw_new, vw_new, gates, cache_cmp_k, cache_cmp_v,
               cache_slc_k, cache_slc_v, win_k, win_v, page_table, layer, cmp_k, cmp_v):
    b, tn = q.shape[:2]
    past_len = page_table.shape[1] * PAGE_SIZE
    t_total = past_len + tn
    kc = compress(jnp.concatenate([gather_pages(cache_cmp_k, layer, page_table), kc_new], 1), *cmp_k)
    vc = compress(jnp.concatenate([gather_pages(cache_cmp_v, layer, page_table), vc_new], 1), *cmp_v)
    bidx = jnp.arange(b)[:, None, None, None, None]
    gidx = jnp.arange(G_NSA)[None, None, :, None, None]

    def sel_fn(spos):
        in_past = (spos < past_len)[..., None]
        pp = jnp.clip(spos, 0, past_len - 1)
        page = page_table[bidx, pp // PAGE_SIZE]
        off = pp % PAGE_SIZE
        pn = jnp.clip(spos - past_len, 0, tn - 1)
        k = jnp.where(in_past, cache_slc_k[layer, page, off, gidx], ks_new[bidx, pn, gidx])
        v = jnp.where(in_past, cache_slc_v[layer, page, off, gidx], vs_new[bidx, pn, gidx])
        return k, v

    wbuf = win_k.shape[1]
    kw = jnp.concatenate([win_k, kw_new], 1)
    vw = jnp.concatenate([win_v, vw_new], 1)
    wpos = past_len - wbuf + jnp.arange(wbuf + tn, dtype=jnp.int32)
    qpos = past_len + jnp.arange(tn, dtype=jnp.int32)
    o = nsa_attend(q, qpos, kc, vc, sel_fn, kw, vw, wpos, gates, -(-t_total // L_SLC))
    keep = min(WINDOW, t_total)
    return o, kw[:, wbuf + tn - keep:], vw[:, wbuf + tn - keep:]


def diff_attend(q1, q2, k, v, qpos, kpos, lam, subln_g, lam_init):
    b, tq = q1.shape[:2]
    scale = DH_DIFF ** -0.5
    mask = (kpos[None, :] <= qpos[:, None])[None, None]
    k1, k2 = k[..., :DH_DIFF], k[..., DH_DIFF:]
    a1 = masked_softmax(jnp.einsum('bthd,bshd->bhts', q1, k1).astype(jnp.float32) * scale, mask)
    a2 = masked_softmax(jnp.einsum('bthd,bshd->bhts', q2, k2).astype(jnp.float32) * scale, mask)
    o = jnp.einsum('bhts,bshd->bthd', (a1 - lam * a2).astype(v.dtype), v).astype(jnp.float32)
    o = o * lax.rsqrt(jnp.mean(jnp.square(o), -1, keepdims=True) + LN_EPS) * subln_g.astype(jnp.float32)
    return (o * (1.0 - lam_init)).astype(v.dtype).reshape(b, tq, H_DIFF * DV_DIFF)


def diff_prompt(q1, q2, k, v, lam, subln_g, lam_init):
    b, t = q1.shape[:2]
    n_blk = t // Q_BLOCK
    kpos = jnp.arange(t, dtype=jnp.int32)
    qb1 = q1.reshape(b, n_blk, Q_BLOCK, H_DIFF, DH_DIFF).swapaxes(0, 1)
    qb2 = q2.reshape(b, n_blk, Q_BLOCK, H_DIFF, DH_DIFF).swapaxes(0, 1)

    def one_block(args):
        i, a, c = args
        qpos = i * Q_BLOCK + jnp.arange(Q_BLOCK, dtype=jnp.int32)
        return diff_attend(a, c, k, v, qpos, kpos, lam, subln_g, lam_init)

    out = lax.map(one_block, (jnp.arange(n_blk, dtype=jnp.int32), qb1, qb2))
    return out.swapaxes(0, 1).reshape(b, t, H_DIFF * DV_DIFF)


def mixer_output(o_nsa, o_diff, g_m, w_br_nsa, w_br_diff, w_out):
    m = g_m[:, :, 0] * (o_nsa @ w_br_nsa) + g_m[:, :, 1] * (o_diff @ w_br_diff)
    return m @ w_out


def peer_ffn(xf, wq, subkeys, u, v):
    n = xf.shape[0]
    q = (xf @ wq).reshape(n, PEER_HEADS, 2, PEER_DQ // 2)
    s = jnp.einsum('nhpc,hpkc->nhpk', q, subkeys).astype(jnp.float32)
    s1, i1 = lax.top_k(s[:, :, 0], PEER_TOPK)
    s2, i2 = lax.top_k(s[:, :, 1], PEER_TOPK)
    cand = (s1[..., :, None] + s2[..., None, :]).reshape(n, PEER_HEADS, PEER_TOPK * PEER_TOPK)
    cid = (i1[..., :, None] * N_KEYS + i2[..., None, :]).reshape(n, PEER_HEADS, PEER_TOPK * PEER_TOPK)
    top, sel = lax.top_k(cand, PEER_TOPK)
    eid = jnp.take_along_axis(cid, sel, axis=-1)
    g = jax.nn.softmax(top, axis=-1)
    act = jax.nn.gelu(jnp.einsum('nd,nhkd->nhk', xf, u[eid]), approximate=False)
    return jnp.einsum('nhk,nhkd->nd', (g * act.astype(jnp.float32)).astype(xf.dtype), v[eid])


def _ln_residual_kernel(a_ref, f_ref, g_ref, b_ref, o_ref):
    x = ALPHA * a_ref[...] + f_ref[...]
    mu = jnp.mean(x, axis=-1, keepdims=True)
    xc = x - mu
    var = jnp.mean(xc * xc, axis=-1, keepdims=True)
    o_ref[...] = xc * lax.rsqrt(var + LN_EPS) * g_ref[...] + b_ref[...]


def ln_residual(a, f, g, b, *, tm=256):
    n, d = a.shape
    tm = min(tm, n)
    return pl.pallas_call(
        _ln_residual_kernel,
        grid=(n // tm,),
        in_specs=[pl.BlockSpec((tm, d), lambda i: (i, 0)),
                  pl.BlockSpec((tm, d), lambda i: (i, 0)),
                  pl.BlockSpec((1, d), lambda i: (0, 0)),
                  pl.BlockSpec((1, d), lambda i: (0, 0))],
        out_specs=pl.BlockSpec((tm, d), lambda i: (i, 0)),
        out_shape=jax.ShapeDtypeStruct((n, d), jnp.float32),
        name="ln_residual",
    )(a, f, g.reshape(1, d), b.reshape(1, d))


def post_block(x, mix, ln1_g, ln1_b, ln2_g, ln2_b, peer_w):
    b, t, d = x.shape
    n = b * t
    h = ln_residual(x.reshape(n, d), mix.reshape(n, d), ln1_g, ln1_b)
    n_blk = -(-n // Q_BLOCK)
    hf = jnp.pad(h, ((0, n_blk * Q_BLOCK - n), (0, 0)))
    f = lax.map(lambda blk: peer_ffn(blk, *peer_w), hf.reshape(n_blk, Q_BLOCK, d))
    f = f.reshape(n_blk * Q_BLOCK, d)[:n]
    return ln_residual(h, f, ln2_g, ln2_b).reshape(b, t, d)


def kernel(x_prompt, x_sample, cache_diff_k, cache_diff_v, cache_nsa_cmp_k, cache_nsa_cmp_v,
           cache_nsa_slc_k, cache_nsa_slc_v, state_nsa_win_k, state_nsa_win_v, page_table,
           w_in, cmp_w1_k, cmp_pe_k, cmp_w2_k, cmp_w1_v, cmp_pe_v, cmp_w2_v,
           lambda_q1, lambda_k1, lambda_q2, lambda_k2, diff_subln_g,
           w_br_nsa, w_br_diff, w_out, ln1_g, ln1_b,
           peer_wq, peer_subkeys, peer_u, peer_v, ln2_g, ln2_b):
    past_len = page_table.shape[1] * PAGE_SIZE
    seq, dec_seq = x_prompt.shape[1], x_sample.shape[1]
    pos_p = jnp.arange(seq, dtype=jnp.int32)
    pos_s = past_len + jnp.arange(dec_seq, dtype=jnp.int32)
    l = 0
    lam_init = 0.8 - 0.6 * math.exp(-0.3 * l)
    lam = (jnp.exp(jnp.sum(lambda_q1[l] * lambda_k1[l]))
           - jnp.exp(jnp.sum(lambda_q2[l] * lambda_k2[l])) + lam_init)
    cmp_k = (cmp_w1_k[l], cmp_pe_k[l], cmp_w2_k[l])
    cmp_v = (cmp_w1_v[l], cmp_pe_v[l], cmp_w2_v[l])
    peer_w = (peer_wq[l], peer_subkeys[l], peer_u[l], peer_v[l])

    q_n, kc, vc, ks, vs, kw, vw, g_n, q1, q2, kd, vd, g_m = mixer_project(x_prompt, pos_p, w_in[l])
    o_n = nsa_prompt(q_n, kc, vc, ks, vs, kw, vw, g_n, cmp_k, cmp_v)
    o_d = diff_prompt(q1, q2, kd, vd, lam, diff_subln_g[l], lam_init)
    mix = mixer_output(o_n, o_d, g_m, w_br_nsa[l], w_br_diff[l], w_out[l])
    keep = min(WINDOW, seq)
    outs_p = (kd, vd, kc, vc, ks, vs, kw[:, seq - keep:], vw[:, seq - keep:])
    xp = post_block(x_prompt, mix, ln1_g[l], ln1_b[l], ln2_g[l], ln2_b[l], peer_w)

    q_n, kc, vc, ks, vs, kw, vw, g_n, q1, q2, kd, vd, g_m = mixer_project(x_sample, pos_s, w_in[l])
    o_n, win_k_new, win_v_new = nsa_sample(q_n, kc, vc, ks, vs, kw, vw, g_n,
                                           cache_nsa_cmp_k, cache_nsa_cmp_v, cache_nsa_slc_k, cache_nsa_slc_v,
                                           state_nsa_win_k[l], state_nsa_win_v[l], page_table, l, cmp_k, cmp_v)
    k_all = jnp.concatenate([gather_pages(cache_diff_k, l, page_table), kd], 1)
    v_all = jnp.concatenate([gather_pages(cache_diff_v, l, page_table), vd], 1)
    kpos = jnp.arange(past_len + dec_seq, dtype=jnp.int32)
    o_d = diff_attend(q1, q2, k_all, v_all, pos_s, kpos, lam, diff_subln_g[l], lam_init)
    mix = mixer_output(o_n, o_d, g_m, w_br_nsa[l], w_br_diff[l], w_out[l])
    outs_s = (kd, vd, kc, vc, ks, vs, win_k_new, win_v_new)
    xs = post_block(x_sample, mix, ln1_g[l], ln1_b[l], ln2_g[l], ln2_b[l], peer_w)

    return (xp, xs) + tuple(a[None] for a in outs_p) + tuple(a[None] for a in outs_s)
```

```python
import functools
import math

import jax
import jax.numpy as jnp
from jax import lax
import numpy as np
from jax.experimental import pallas as pl
from jax.experimental.pallas import tpu as pltpu

D_MODEL = 2048
DEPTH = 1
PAGE_SIZE = 128
H_NSA = 16
G_NSA = 2
HPG_NSA = H_NSA // G_NSA
DH_NSA = 64
L_CMP = 32
STRIDE_CMP = 16
CMP_HID = 2 * DH_NSA
L_SLC = 64
N_SEL = 16
WINDOW = 512
H_DIFF = 8
DH_DIFF = 64
DV_DIFF = 2 * DH_DIFF
N_KEYS = 128
N_EXPERTS = N_KEYS * N_KEYS
PEER_HEADS = 8
PEER_DQ = 256
PEER_TOPK = 16
ROPE_THETA = 10000.0
Q_BLOCK = 128
LN_EPS = 1e-5
NEG_INF = -1e30
SEL_BONUS = 1e6
ALPHA = (2.0 * DEPTH) ** 0.25

W_NSA_Q = H_NSA * DH_NSA
W_NSA_KV = G_NSA * DH_NSA
W_NSA_GATE = H_NSA * 3
W_DIFF_QK = H_DIFF * 2 * DH_DIFF
W_DIFF_V = H_DIFF * DV_DIFF
COL_SIZES = (W_NSA_Q, W_NSA_KV, W_NSA_KV, W_NSA_KV, W_NSA_KV, W_NSA_KV, W_NSA_KV, W_NSA_GATE,
             W_DIFF_QK, W_DIFF_QK, W_DIFF_V, 2 * D_MODEL)


def layer_norm(x, g, b):
    xf = x.astype(jnp.float32)
    mu = xf.mean(-1, keepdims=True)
    var = jnp.square(xf - mu).mean(-1, keepdims=True)
    y = (xf - mu) * lax.rsqrt(var + LN_EPS) * g.astype(jnp.float32) + b.astype(jnp.float32)
    return y.astype(x.dtype)


def rope(x, pos):
    d = x.shape[-1]
    half = d // 2
    inv = ROPE_THETA ** (-jnp.arange(half, dtype=jnp.float32) / half)
    ang = pos.astype(jnp.float32)[:, None] * inv[None, :]
    cos, sin = jnp.cos(ang)[None, :, None, :], jnp.sin(ang)[None, :, None, :]
    xf = x.astype(jnp.float32)
    x1, x2 = xf[..., :half], xf[..., half:]
    return jnp.concatenate([x1 * cos - x2 * sin, x1 * sin + x2 * cos], -1).astype(x.dtype)


def masked_softmax(s, mask):
    s = jnp.where(mask, s.astype(jnp.float32), NEG_INF)
    return jnp.where(mask, jax.nn.softmax(s, axis=-1), 0.0)


def split_cols(z):
    out, start = [], 0
    for size in COL_SIZES:
        out.append(z[..., start:start + size])
        start += size
    return out


def gather_pages(cache, layer, page_table):
    b, n_pages = page_table.shape
    rows = cache[layer, page_table]
    return rows.reshape((b, n_pages * PAGE_SIZE) + cache.shape[3:])


def mixer_project(x, pos, w_in):
    b, t, _ = x.shape
    q_n, kc, vc, ks, vs, kw, vw, g_n, q_d, k_d, v_d, g_m = split_cols(x @ w_in)
    kvh = lambda a: a.reshape(b, t, G_NSA, DH_NSA)
    q_n = rope(q_n.reshape(b, t, H_NSA, DH_NSA), pos).reshape(b, t, G_NSA, HPG_NSA, DH_NSA)
    kc, ks, kw = rope(kvh(kc), pos), rope(kvh(ks), pos), rope(kvh(kw), pos)
    vc, vs, vw = kvh(vc), kvh(vs), kvh(vw)
    g_n = jax.nn.sigmoid(g_n.reshape(b, t, H_NSA, 3))
    q_d = q_d.reshape(b, t, H_DIFF, 2, DH_DIFF)
    k_d = k_d.reshape(b, t, H_DIFF, 2, DH_DIFF)
    q1, q2 = rope(q_d[:, :, :, 0], pos), rope(q_d[:, :, :, 1], pos)
    k_d = jnp.concatenate([rope(k_d[:, :, :, 0], pos), rope(k_d[:, :, :, 1], pos)], -1)
    v_d = v_d.reshape(b, t, H_DIFF, DV_DIFF)
    g_m = jax.nn.sigmoid(g_m.reshape(b, t, 2, D_MODEL))
    return (q_n, kc, vc, ks, vs, kw, vw, g_n, q1, q2, k_d, v_d, g_m)


def compress(rows, w1, pe, w2):
    b, t = rows.shape[:2]
    nc = (t - L_CMP) // STRIDE_CMP + 1
    r = rows[:, :STRIDE_CMP * (nc + 1)].reshape(b, nc + 1, STRIDE_CMP, G_NSA, DH_NSA)
    first = jnp.einsum('bnlgd,lde->bnge', r, w1[:STRIDE_CMP])
    second = jnp.einsum('bnlgd,lde->bnge', r, w1[STRIDE_CMP:])
    pe_term = jnp.einsum('ld,lde->e', pe, w1)
    hid = jax.nn.gelu(first[:, :-1] + second[:, 1:] + pe_term, approximate=False)
    return jnp.einsum('bnge,ed->bngd', hid, w2)


def nsa_attend(q, qpos, kc, vc, sel_fn, kw, vw, wpos, gates, n_slc):
    b, tq = q.shape[:2]
    nc = kc.shape[1]
    scale = DH_NSA ** -0.5
    t = qpos[:, None]
    cend = jnp.arange(nc, dtype=jnp.int32) * STRIDE_CMP + (L_CMP - 1)
    s = jnp.einsum('btghd,bngd->btghn', q, kc).astype(jnp.float32) * scale
    p_cmp = masked_softmax(s, (cend[None, :] <= t)[None, :, None, None, :])
    o_cmp = jnp.einsum('btghn,bngd->btghd', p_cmp.astype(vc.dtype), vc)
    ratio = L_SLC // STRIDE_CMP
    imp = jnp.pad(p_cmp.sum(axis=3), ((0, 0), (0, 0), (0, 0), (0, n_slc * ratio - nc)))
    imp = imp.reshape(b, tq, G_NSA, n_slc, ratio).sum(-1)
    blk = jnp.arange(n_slc, dtype=jnp.int32)[None, :]
    cur = (qpos // L_SLC)[:, None]
    forced = (blk == 0) | (blk == cur) | (blk == cur - 1)
    future = blk * L_SLC > t
    score = jnp.where(future[None, :, None, :], -SEL_BONUS,
                      imp + jnp.where(forced, SEL_BONUS, 0.0)[None, :, None, :])
    vals, idx = lax.top_k(score, min(N_SEL, n_slc))
    valid = vals > -0.5 * SEL_BONUS
    spos = idx[..., None] * L_SLC + jnp.arange(L_SLC, dtype=jnp.int32)
    ks, vs = sel_fn(spos)
    smask = (valid[..., None] & (spos <= qpos[None, :, None, None, None])).reshape(b, tq, G_NSA, -1)
    ks = ks.reshape(b, tq, G_NSA, -1, DH_NSA)
    vs = vs.reshape(b, tq, G_NSA, -1, DH_NSA)
    s = jnp.einsum('btghd,btgkd->btghk', q, ks).astype(jnp.float32) * scale
    p = masked_softmax(s, smask[:, :, :, None, :])
    o_slc = jnp.einsum('btghk,btgkd->btghd', p.astype(vs.dtype), vs)
    dist = t - wpos[None, :]
    wmask = (dist >= 0) & (dist <= WINDOW) & (wpos[None, :] >= 0)
    s = jnp.einsum('btghd,bsgd->btghs', q, kw).astype(jnp.float32) * scale
    p = masked_softmax(s, wmask[None, :, None, None, :])
    o_win = jnp.einsum('btghs,bsgd->btghd', p.astype(vw.dtype), vw)
    g = gates.reshape(b, tq, G_NSA, HPG_NSA, 3)
    o = g[..., 0:1] * o_cmp + g[..., 1:2] * o_slc + g[..., 2:3] * o_win
    return o.reshape(b, tq, H_NSA * DH_NSA)


def nsa_prompt(q, kc_rows, vc_rows, ks_rows, vs_rows, kw_rows, vw_rows, gates, cmp_k, cmp_v):
    b, t = q.shape[:2]
    kc, vc = compress(kc_rows, *cmp_k), compress(vc_rows, *cmp_v)
    n_slc = -(-t // L_SLC)
    bidx = jnp.arange(b)[:, None, None, None, None]
    gidx = jnp.arange(G_NSA)[None, None, :, None, None]

    def sel_fn(spos):
        p = jnp.clip(spos, 0, t - 1)
        return ks_rows[bidx, p, gidx], vs_rows[bidx, p, gidx]

    pad = ((0, 0), (WINDOW, 0), (0, 0), (0, 0))
    kw_pad, vw_pad = jnp.pad(kw_rows, pad), jnp.pad(vw_rows, pad)
    n_blk = t // Q_BLOCK
    q_blk = q.reshape(b, n_blk, Q_BLOCK, G_NSA, HPG_NSA, DH_NSA).swapaxes(0, 1)
    g_blk = gates.reshape(b, n_blk, Q_BLOCK, H_NSA, 3).swapaxes(0, 1)

    def one_block(args):
        i, qb, gb = args
        q0 = i * Q_BLOCK
        qpos = q0 + jnp.arange(Q_BLOCK, dtype=jnp.int32)
        wpos = q0 - WINDOW + jnp.arange(WINDOW + Q_BLOCK, dtype=jnp.int32)
        kw = lax.dynamic_slice_in_dim(kw_pad, q0, WINDOW + Q_BLOCK, axis=1)
        vw = lax.dynamic_slice_in_dim(vw_pad, q0, WINDOW + Q_BLOCK, axis=1)
        return nsa_attend(qb, qpos, kc, vc, sel_fn, kw, vw, wpos, gb, n_slc)

    out = lax.map(one_block, (jnp.arange(n_blk, dtype=jnp.int32), q_blk, g_blk))
    return out.swapaxes(0, 1).reshape(b, t, H_NSA * DH_NSA)


def nsa_sample(q, kc_new, vc_new, ks_new, vs_new, kw_new, vw_new, gates, cache_cmp_k, cache_cmp_v,
               cache_slc_k, cache_slc_v, win_k, win_v, page_table, layer, cmp_k, cmp_v):
    b, tn = q.shape[:2]
    past_len = page_table.shape[1] * PAGE_SIZE
    t_total = past_len + tn
    kc = compress(jnp.concatenate([gather_pages(cache_cmp_k, layer, page_table), kc_new], 1), *cmp_k)
    vc = compress(jnp.concatenate([gather_pages(cache_cmp_v, layer, page_table), vc_new], 1), *cmp_v)
    bidx = jnp.arange(b)[:, None, None, None, None]
    gidx = jnp.arange(G_NSA)[None, None, :, None, None]

    def sel_fn(spos):
        in_past = (spos < past_len)[..., None]
        pp = jnp.clip(spos, 0, past_len - 1)
        page = page_table[bidx, pp // PAGE_SIZE]
        off = pp % PAGE_SIZE
        pn = jnp.clip(spos - past_len, 0, tn - 1)
        k = jnp.where(in_past, cache_slc_k[layer, page, off, gidx], ks_new[bidx, pn, gidx])
        v = jnp.where(in_past, cache_slc_v[layer, page, off, gidx], vs_new[bidx, pn, gidx])
        return k, v

    wbuf = win_k.shape[1]
    kw = jnp.concatenate([win_k, kw_new], 1)
    vw = jnp.concatenate([win_v, vw_new], 1)
    wpos = past_len - wbuf + jnp.arange(wbuf + tn, dtype=jnp.int32)
    qpos = past_len + jnp.arange(tn, dtype=jnp.int32)
    o = nsa_attend(q, qpos, kc, vc, sel_fn, kw, vw, wpos, gates, -(-t_total // L_SLC))
    keep = min(WINDOW, t_total)
    return o, kw[:, wbuf + tn - keep:], vw[:, wbuf + tn - keep:]


def diff_attend(q1, q2, k, v, qpos, kpos, lam, subln_g, lam_init):
    b, tq = q1.shape[:2]
    scale = DH_DIFF ** -0.5
    mask = (kpos[None, :] <= qpos[:, None])[None, None]
    k1, k2 = k[..., :DH_DIFF], k[..., DH_DIFF:]
    a1 = masked_softmax(jnp.einsum('bthd,bshd->bhts', q1, k1).astype(jnp.float32) * scale, mask)
    a2 = masked_softmax(jnp.einsum('bthd,bshd->bhts', q2, k2).astype(jnp.float32) * scale, mask)
    o = jnp.einsum('bhts,bshd->bthd', (a1 - lam * a2).astype(v.dtype), v).astype(jnp.float32)
    o = o * lax.rsqrt(jnp.mean(jnp.square(o), -1, keepdims=True) + LN_EPS) * subln_g.astype(jnp.float32)
    return (o * (1.0 - lam_init)).astype(v.dtype).reshape(b, tq, H_DIFF * DV_DIFF)


def diff_prompt(q1, q2, k, v, lam, subln_g, lam_init):
    b, t = q1.shape[:2]
    n_blk = t // Q_BLOCK
    kpos = jnp.arange(t, dtype=jnp.int32)
    qb1 = q1.reshape(b, n_blk, Q_BLOCK, H_DIFF, DH_DIFF).swapaxes(0, 1)
    qb2 = q2.reshape(b, n_blk, Q_BLOCK, H_DIFF, DH_DIFF).swapaxes(0, 1)

    def one_block(args):
        i, a, c = args
        qpos = i * Q_BLOCK + jnp.arange(Q_BLOCK, dtype=jnp.int32)
        return diff_attend(a, c, k, v, qpos, kpos, lam, subln_g, lam_init)

    out = lax.map(one_block, (jnp.arange(n_blk, dtype=jnp.int32), qb1, qb2))
    return out.swapaxes(0, 1).reshape(b, t, H_DIFF * DV_DIFF)


def mixer_output(o_nsa, o_diff, g_m, w_br_nsa, w_br_diff, w_out):
    m = g_m[:, :, 0] * (o_nsa @ w_br_nsa) + g_m[:, :, 1] * (o_diff @ w_br_diff)
    return m @ w_out


def peer_ffn(xf, wq, subkeys, u, v):
    n = xf.shape[0]
    q = (xf @ wq).reshape(n, PEER_HEADS, 2, PEER_DQ // 2)
    s = jnp.einsum('nhpc,hpkc->nhpk', q, subkeys).astype(jnp.float32)
    s1, i1 = lax.top_k(s[:, :, 0], PEER_TOPK)
    s2, i2 = lax.top_k(s[:, :, 1], PEER_TOPK)
    cand = (s1[..., :, None] + s2[..., None, :]).reshape(n, PEER_HEADS, PEER_TOPK * PEER_TOPK)
    cid = (i1[..., :, None] * N_KEYS + i2[..., None, :]).reshape(n, PEER_HEADS, PEER_TOPK * PEER_TOPK)
    top, sel = lax.top_k(cand, PEER_TOPK)
    eid = jnp.take_along_axis(cid, sel, axis=-1)
    g = jax.nn.softmax(top, axis=-1)
    act = jax.nn.gelu(jnp.einsum('nd,nhkd->nhk', xf, u[eid]), approximate=False)
    return jnp.einsum('nhk,nhkd->nd', (g * act.astype(jnp.float32)).astype(xf.dtype), v[eid])


_BF16 = jnp.bfloat16
_F32 = jnp.float32
_VMEM_LIMIT = 56 * 1024 * 1024


def _dot_nt(a, b):
    return lax.dot_general(a, b, (((1,), (1,)), ((), ())), preferred_element_type=_F32)


def _dot(a, b):
    return jnp.dot(a, b, preferred_element_type=_F32)


def _dot_exact01(a, m01):
    hi = a.astype(_BF16)
    r1 = a - hi.astype(_F32)
    mid = r1.astype(_BF16)
    lo = (r1 - mid.astype(_F32)).astype(_BF16)
    return _dot(hi, m01) + _dot(mid, m01) + _dot(lo, m01)


def _gelu_erf(x):
    return 0.5 * x * (1.0 + lax.erf(x * (2.0 ** -0.5)))


def _softmax_rows(s, mask):
    s = jnp.where(mask, s, NEG_INF)
    m = jnp.max(s, axis=-1, keepdims=True)
    e = jnp.where(mask, jnp.exp(s - m), 0.0)
    l = jnp.sum(e, axis=-1, keepdims=True)
    return e / jnp.where(l > 0.0, l, 1.0)


def _compress_kernel(r_ref, w1a_ref, w1b_ref, pe_ref, w1_ref, w2_ref, o_ref):
    r = r_ref[...]
    first = _dot(r, w1a_ref[...])
    second = _dot(r, w1b_ref[...])
    nb = first.shape[0]
    second_next = jnp.concatenate([second[1:], jnp.zeros((1, CMP_HID), _F32)], axis=0)
    pe_term = _dot(pe_ref[...], w1_ref[...])[0:1]
    hid = _gelu_erf(first + second_next + pe_term)
    o_ref[...] = _dot(hid.astype(_BF16), w2_ref[...])


def compress_pallas(rows, w1, pe, w2):
    t = rows.shape[0]
    nb = t // STRIDE_CMP
    r = rows.reshape(nb, STRIDE_CMP, G_NSA, DH_NSA).transpose(2, 0, 1, 3)
    r = r.reshape(G_NSA, nb, STRIDE_CMP * DH_NSA).astype(_BF16)
    w1f = w1.reshape(L_CMP * DH_NSA, CMP_HID).astype(_BF16)
    half = STRIDE_CMP * DH_NSA
    pe8 = jnp.broadcast_to(pe.reshape(1, L_CMP * DH_NSA), (8, L_CMP * DH_NSA)).astype(_BF16)
    full = lambda shape: pl.BlockSpec(shape, lambda g: (0,) * len(shape))
    return pl.pallas_call(
        _compress_kernel,
        grid=(G_NSA,),
        in_specs=[pl.BlockSpec((None, nb, half), lambda g: (g, 0, 0)),
                  full((half, CMP_HID)), full((half, CMP_HID)),
                  full((8, 2 * half)), full((2 * half, CMP_HID)), full((CMP_HID, DH_NSA))],
        out_specs=pl.BlockSpec((None, nb, DH_NSA), lambda g: (g, 0, 0)),
        out_shape=jax.ShapeDtypeStruct((G_NSA, nb, DH_NSA), _F32),
        name="nsa_compress",
    )(r, w1f[:half], w1f[half:], pe8, w1f, w2.astype(_BF16))


_NSA_TQ = 128
_NSA_TK = 512


def _nsa_prompt_kernel(q_ref, kc_ref, vc_ref, ks_ref, vs_ref, kw_ref, vw_ref, g_ref, o_ref, st_ref,
                       *, nc, t_total):
    tq, tk, hh = _NSA_TQ, _NSA_TK, HPG_NSA
    q0 = pl.program_id(1) * tq
    qt = q_ref[...]
    q = jnp.concatenate([qt[:, h * DH_NSA:(h + 1) * DH_NSA] for h in range(hh)], axis=0)
    q = (q * (DH_NSA ** -0.5)).astype(_BF16)
    tpos = q0 + lax.broadcasted_iota(jnp.int32, (tq, 1), 0)

    ncp = kc_ref.shape[0]
    s = _dot_nt(q, kc_ref[...]).reshape(hh, tq, ncp)
    n_idx = lax.broadcasted_iota(jnp.int32, (tq, ncp), 1)
    cmask = ((n_idx * STRIDE_CMP + (L_CMP - 1)) <= tpos) & (n_idx < nc)
    p = _softmax_rows(s, cmask[None])
    o_cmp = _dot(p.reshape(hh * tq, ncp).astype(_BF16), vc_ref[...])
    ratio = L_SLC // STRIDE_CMP
    n_slc = ncp // ratio
    pool = (lax.broadcasted_iota(jnp.int32, (ncp, n_slc), 0) // ratio
            == lax.broadcasted_iota(jnp.int32, (ncp, n_slc), 1)).astype(_BF16)
    imp = _dot_exact01(jnp.sum(p, axis=0), pool)

    blk = lax.broadcasted_iota(jnp.int32, (tq, n_slc), 1)
    cur = tpos // L_SLC
    forced = (blk == 0) | (blk == cur) | (blk == cur - 1)
    future = blk * L_SLC > tpos
    score = jnp.where(future, -SEL_BONUS, imp + jnp.where(forced, SEL_BONUS, 0.0))
    st_ref[...] = score.T
    st = st_ref[...]
    b_idx = lax.broadcasted_iota(jnp.int32, (n_slc, tq), 0)

    def rank_body(bp, cnt):
        row = st_ref[pl.ds(bp, 1), :]
        ahead = (row > st) | ((row == st) & (b_idx > bp))
        return cnt + jnp.where(ahead, 1.0, 0.0)

    cnt = lax.fori_loop(0, n_slc, rank_body, jnp.zeros((n_slc, tq), _F32), unroll=8)
    sel_t = (cnt < float(N_SEL)) & (st > -0.5 * SEL_BONUS)
    sel = sel_t.astype(_F32).T.astype(_BF16)

    def slc_body(c, carry):
        m, l, acc = carry
        k0 = pl.multiple_of(c * tk, tk)
        kk = ks_ref[pl.ds(k0, tk), :]
        vv = vs_ref[pl.ds(k0, tk), :]
        sc = _dot_nt(q, kk).reshape(hh, tq, tk)
        kpos = k0 + lax.broadcasted_iota(jnp.int32, (1, tk), 1)
        expand = ((k0 + lax.broadcasted_iota(jnp.int32, (n_slc, tk), 1)) // L_SLC
                  == lax.broadcasted_iota(jnp.int32, (n_slc, tk), 0)).astype(_BF16)
        mk = (_dot(sel, expand) > 0.5) & (kpos <= tpos)
        mk = mk[None]
        sc = jnp.where(mk, sc, NEG_INF)
        m_new = jnp.maximum(m, jnp.max(sc, axis=-1, keepdims=True))
        alpha = jnp.exp(m - m_new)
        pp = jnp.where(mk, jnp.exp(sc - m_new), 0.0)
        l = alpha * l + jnp.sum(pp, axis=-1, keepdims=True)
        acc = acc * alpha.reshape(hh * tq, 1) + _dot(pp.reshape(hh * tq, tk).astype(_BF16), vv)
        return m_new, l, acc

    n_chunks = (q0 + tq + tk - 1) // tk
    m0 = jnp.full((hh, tq, 1), NEG_INF, _F32)
    l0 = jnp.zeros((hh, tq, 1), _F32)
    a0 = jnp.zeros((hh * tq, DH_NSA), _F32)
    _, l_s, acc_s = lax.fori_loop(0, n_chunks, slc_body, (m0, l0, a0))
    o_slc = acc_s / jnp.where(l_s > 0.0, l_s, 1.0).reshape(hh * tq, 1)

    wlen = WINDOW + tq
    w0 = pl.multiple_of(jnp.maximum(q0 - WINDOW, 0), tq)
    sw = _dot_nt(q, kw_ref[pl.ds(w0, wlen), :]).reshape(hh, tq, wlen)
    dist = tpos - (w0 + lax.broadcasted_iota(jnp.int32, (1, wlen), 1))
    wmask = (dist >= 0) & (dist <= WINDOW)
    pw = _softmax_rows(sw, wmask[None])
    o_win = _dot(pw.reshape(hh * tq, wlen).astype(_BF16), vw_ref[pl.ds(w0, wlen), :])

    g = g_ref[...]
    outs = []
    for h in range(hh):
        rows = slice(h * tq, (h + 1) * tq)
        outs.append(g[:, 3 * h:3 * h + 1] * o_cmp[rows] + g[:, 3 * h + 1:3 * h + 2] * o_slc[rows]
                    + g[:, 3 * h + 2:3 * h + 3] * o_win[rows])
    o_ref[...] = jnp.concatenate(outs, axis=1)


def nsa_prompt_pallas(q, kc, vc, ks_rows, vs_rows, kw_rows, vw_rows, gates, nc):
    t = q.shape[0]
    ncp = kc.shape[1]
    tq = _NSA_TQ
    assert t % _NSA_TK == 0 and t >= WINDOW + tq and ncp * STRIDE_CMP == t
    per_group = lambda a: a.reshape(t, G_NSA, DH_NSA).transpose(1, 0, 2).astype(_BF16)
    g3 = gates.reshape(t, G_NSA, HPG_NSA * 3).transpose(1, 0, 2)
    hw = HPG_NSA * DH_NSA
    kv_spec = pl.BlockSpec((None, t, DH_NSA), lambda g, i: (g, 0, 0))
    c_spec = pl.BlockSpec((None, ncp, DH_NSA), lambda g, i: (g, 0, 0))
    return pl.pallas_call(
        functools.partial(_nsa_prompt_kernel, nc=nc, t_total=t),
        grid=(G_NSA, t // tq),
        in_specs=[pl.BlockSpec((tq, hw), lambda g, i: (i, g)),
                  c_spec, c_spec, kv_spec, kv_spec, kv_spec, kv_spec,
                  pl.BlockSpec((None, tq, HPG_NSA * 3), lambda g, i: (g, i, 0))],
        out_specs=pl.BlockSpec((tq, hw), lambda g, i: (i, g)),
        out_shape=jax.ShapeDtypeStruct((t, H_NSA * DH_NSA), _F32),
        scratch_shapes=[pltpu.VMEM((ncp // (L_SLC // STRIDE_CMP), tq), _F32)],
        compiler_params=pltpu.CompilerParams(
            dimension_semantics=("arbitrary", "arbitrary"), vmem_limit_bytes=_VMEM_LIMIT),
        name="nsa_prompt",
    )(q, kc.astype(_BF16), vc.astype(_BF16), per_group(ks_rows), per_group(vs_rows),
      per_group(kw_rows), per_group(vw_rows), g3)


_DIFF_TQ = 256
_DIFF_TK = 512


def _diff_prompt_kernel(lam_ref, q_ref, k_ref, v_ref, gam_ref, o_ref, *, out_scale):
    tq, tk = _DIFF_TQ, _DIFF_TK
    q0 = pl.program_id(1) * tq
    qt = (q_ref[...] * (DH_DIFF ** -0.5)).astype(_BF16)
    q1, q2 = qt[:, :DH_DIFF], qt[:, DH_DIFF:]
    tpos = q0 + lax.broadcasted_iota(jnp.int32, (tq, 1), 0)

    def body(c, carry):
        m1, l1, a1, m2, l2, a2 = carry
        k0 = pl.multiple_of(c * tk, tk)
        kk = k_ref[pl.ds(k0, tk), :]
        vv = v_ref[pl.ds(k0, tk), :]
        mk = (k0 + lax.broadcasted_iota(jnp.int32, (1, tk), 1)) <= tpos

        def stream(qh, kh, m, l, a):
            sc = jnp.where(mk, _dot_nt(qh, kh), NEG_INF)
            m_new = jnp.maximum(m, jnp.max(sc, axis=-1, keepdims=True))
            alpha = jnp.exp(m - m_new)
            pp = jnp.where(mk, jnp.exp(sc - m_new), 0.0)
            l = alpha * l + jnp.sum(pp, axis=-1, keepdims=True)
            a = alpha * a + _dot(pp.astype(_BF16), vv)
            return m_new, l, a

        m1, l1, a1 = stream(q1, kk[:, :DH_DIFF], m1, l1, a1)
        m2, l2, a2 = stream(q2, kk[:, DH_DIFF:], m2, l2, a2)
        return m1, l1, a1, m2, l2, a2

    n_chunks = (q0 + tq + tk - 1) // tk
    mi = jnp.full((tq, 1), NEG_INF, _F32)
    li = jnp.zeros((tq, 1), _F32)
    ai = jnp.zeros((tq, DV_DIFF), _F32)
    _, l1, a1, _, l2, a2 = lax.fori_loop(0, n_chunks, body, (mi, li, ai, mi, li, ai))
    lam = lam_ref[0:1, 0:1]
    o = a1 / l1 - lam * (a2 / l2)
    o = o * lax.rsqrt(jnp.mean(o * o, axis=-1, keepdims=True) + LN_EPS) * gam_ref[...]
    o_ref[...] = o * out_scale


def diff_prompt_pallas(q, k, v, lam, subln_g, lam_init):
    t = q.shape[0]
    tq = _DIFF_TQ
    assert t % _DIFF_TK == 0
    lam_row = jnp.full((8, 128), lam, _F32)
    hw = 2 * DH_DIFF
    return pl.pallas_call(
        functools.partial(_diff_prompt_kernel, out_scale=1.0 - lam_init),
        grid=(H_DIFF, t // tq),
        in_specs=[pl.BlockSpec((8, 128), lambda h, i: (0, 0)),
                  pl.BlockSpec((tq, hw), lambda h, i: (i, h)),
                  pl.BlockSpec((t, hw), lambda h, i: (0, h)),
                  pl.BlockSpec((t, DV_DIFF), lambda h, i: (0, h)),
                  pl.BlockSpec((1, DV_DIFF), lambda h, i: (0, 0))],
        out_specs=pl.BlockSpec((tq, DV_DIFF), lambda h, i: (i, h)),
        out_shape=jax.ShapeDtypeStruct((t, H_DIFF * DV_DIFF), _F32),
        compiler_params=pltpu.CompilerParams(
            dimension_semantics=("arbitrary", "arbitrary"), vmem_limit_bytes=_VMEM_LIMIT),
        name="diff_prompt",
    )(lam_row, q, k.astype(_BF16), v.astype(_BF16), subln_g.reshape(1, DV_DIFF))


def _ln_residual_kernel(a_ref, f_ref, g_ref, b_ref, o_ref):
    x = ALPHA * a_ref[...] + f_ref[...]
    mu = jnp.mean(x, axis=-1, keepdims=True)
    xc = x - mu
    var = jnp.mean(xc * xc, axis=-1, keepdims=True)
    o_ref[...] = xc * lax.rsqrt(var + LN_EPS) * g_ref[...] + b_ref[...]


def ln_residual(a, f, g, b, *, tm=256):
    n, d = a.shape
    tm = min(tm, n)
    return pl.pallas_call(
        _ln_residual_kernel,
        grid=(n // tm,),
        in_specs=[pl.BlockSpec((tm, d), lambda i: (i, 0)),
                  pl.BlockSpec((tm, d), lambda i: (i, 0)),
                  pl.BlockSpec((1, d), lambda i: (0, 0)),
                  pl.BlockSpec((1, d), lambda i: (0, 0))],
        out_specs=pl.BlockSpec((tm, d), lambda i: (i, 0)),
        out_shape=jax.ShapeDtypeStruct((n, d), jnp.float32),
        name="ln_residual",
    )(a, f, g.reshape(1, d), b.reshape(1, d))


def prompt_attention(q_n, kc, vc, ks, vs, kw, vw, g_n, q1, q2, kd, vd, cmp_k, cmp_v, lam, subln_g, lam_init):
    t = q_n.shape[1]
    flat = lambda a: a.reshape(t, -1)
    nc = (t - L_CMP) // STRIDE_CMP + 1
    kc_c = compress_pallas(flat(kc), *cmp_k)
    vc_c = compress_pallas(flat(vc), *cmp_v)
    o_n = nsa_prompt_pallas(flat(q_n), kc_c, vc_c, flat(ks), flat(vs), flat(kw), flat(vw), flat(g_n), nc)
    q_d = jnp.concatenate([q1, q2], axis=-1)
    o_d = diff_prompt_pallas(flat(q_d), flat(kd), flat(vd), lam, subln_g, lam_init)
    return o_n[None], o_d[None]


def post_block(x, mix, ln1_g, ln1_b, ln2_g, ln2_b, peer_w):
    b, t, d = x.shape
    n = b * t
    h = ln_residual(x.reshape(n, d), mix.reshape(n, d), ln1_g, ln1_b)
    n_blk = -(-n // Q_BLOCK)
    hf = jnp.pad(h, ((0, n_blk * Q_BLOCK - n), (0, 0)))
    f = lax.map(lambda blk: peer_ffn(blk, *peer_w), hf.reshape(n_blk, Q_BLOCK, d))
    f = f.reshape(n_blk * Q_BLOCK, d)[:n]
    return ln_residual(h, f, ln2_g, ln2_b).reshape(b, t, d)


def kernel(x_prompt, x_sample, cache_diff_k, cache_diff_v, cache_nsa_cmp_k, cache_nsa_cmp_v,
           cache_nsa_slc_k, cache_nsa_slc_v, state_nsa_win_k, state_nsa_win_v, page_table,
           w_in, cmp_w1_k, cmp_pe_k, cmp_w2_k, cmp_w1_v, cmp_pe_v, cmp_w2_v,
           lambda_q1, lambda_k1, lambda_q2, lambda_k2, diff_subln_g,
           w_br_nsa, w_br_diff, w_out, ln1_g, ln1_b,
           peer_wq, peer_subkeys, peer_u, peer_v, ln2_g, ln2_b):
    past_len = page_table.shape[1] * PAGE_SIZE
    seq, dec_seq = x_prompt.shape[1], x_sample.shape[1]
    pos_p = jnp.arange(seq, dtype=jnp.int32)
    pos_s = past_len + jnp.arange(dec_seq, dtype=jnp.int32)
    l = 0
    lam_init = 0.8 - 0.6 * math.exp(-0.3 * l)
    lam = (jnp.exp(jnp.sum(lambda_q1[l] * lambda_k1[l]))
           - jnp.exp(jnp.sum(lambda_q2[l] * lambda_k2[l])) + lam_init)
    cmp_k = (cmp_w1_k[l], cmp_pe_k[l], cmp_w2_k[l])
    cmp_v = (cmp_w1_v[l], cmp_pe_v[l], cmp_w2_v[l])
    peer_w = (peer_wq[l], peer_subkeys[l], peer_u[l], peer_v[l])

    q_n, kc, vc, ks, vs, kw, vw, g_n, q1, q2, kd, vd, g_m = mixer_project(x_prompt, pos_p, w_in[l])
    o_n, o_d = prompt_attention(q_n, kc, vc, ks, vs, kw, vw, g_n, q1, q2, kd, vd, cmp_k, cmp_v,
                                lam, diff_subln_g[l], lam_init)
    mix = mixer_output(o_n, o_d, g_m, w_br_nsa[l], w_br_diff[l], w_out[l])
    keep = min(WINDOW, seq)
    outs_p = (kd, vd, kc, vc, ks, vs, kw[:, seq - keep:], vw[:, seq - keep:])
    xp = post_block(x_prompt, mix, ln1_g[l], ln1_b[l], ln2_g[l], ln2_b[l], peer_w)

    q_n, kc, vc, ks, vs, kw, vw, g_n, q1, q2, kd, vd, g_m = mixer_project(x_sample, pos_s, w_in[l])
    o_n, win_k_new, win_v_new = nsa_sample(q_n, kc, vc, ks, vs, kw, vw, g_n,
                                           cache_nsa_cmp_k, cache_nsa_cmp_v, cache_nsa_slc_k, cache_nsa_slc_v,
                                           state_nsa_win_k[l], state_nsa_win_v[l], page_table, l, cmp_k, cmp_v)
    k_all = jnp.concatenate([gather_pages(cache_diff_k, l, page_table), kd], 1)
    v_all = jnp.concatenate([gather_pages(cache_diff_v, l, page_table), vd], 1)
    kpos = jnp.arange(past_len + dec_seq, dtype=jnp.int32)
    o_d = diff_attend(q1, q2, k_all, v_all, pos_s, kpos, lam, diff_subln_g[l], lam_init)
    mix = mixer_output(o_n, o_d, g_m, w_br_nsa[l], w_br_diff[l], w_out[l])
    outs_s = (kd, vd, kc, vc, ks, vs, win_k_new, win_v_new)
    xs = post_block(x_sample, mix, ln1_g[l], ln1_b[l], ln2_g[l], ln2_b[l], peer_w)

    return (xp, xs) + tuple(a[None] for a in outs_p) + tuple(a[None] for a in outs_s)
```

```python
import functools
import math

import jax
import jax.numpy as jnp
from jax import lax
import numpy as np
from jax.experimental import pallas as pl
from jax.experimental.pallas import tpu as pltpu

D_MODEL = 2048
DEPTH = 1
PAGE_SIZE = 128
H_NSA = 16
G_NSA = 2
HPG_NSA = H_NSA // G_NSA
DH_NSA = 64
L_CMP = 32
STRIDE_CMP = 16
CMP_HID = 2 * DH_NSA
L_SLC = 64
N_SEL = 16
WINDOW = 512
H_DIFF = 8
DH_DIFF = 64
DV_DIFF = 2 * DH_DIFF
N_KEYS = 128
N_EXPERTS = N_KEYS * N_KEYS
PEER_HEADS = 8
PEER_DQ = 256
PEER_TOPK = 16
ROPE_THETA = 10000.0
Q_BLOCK = 128
LN_EPS = 1e-5
NEG_INF = -1e30
SEL_BONUS = 1e6
ALPHA = (2.0 * DEPTH) ** 0.25

W_NSA_Q = H_NSA * DH_NSA
W_NSA_KV = G_NSA * DH_NSA
W_NSA_GATE = H_NSA * 3
W_DIFF_QK = H_DIFF * 2 * DH_DIFF
W_DIFF_V = H_DIFF * DV_DIFF
COL_SIZES = (W_NSA_Q, W_NSA_KV, W_NSA_KV, W_NSA_KV, W_NSA_KV, W_NSA_KV, W_NSA_KV, W_NSA_GATE,
             W_DIFF_QK, W_DIFF_QK, W_DIFF_V, 2 * D_MODEL)


def layer_norm(x, g, b):
    xf = x.astype(jnp.float32)
    mu = xf.mean(-1, keepdims=True)
    var = jnp.square(xf - mu).mean(-1, keepdims=True)
    y = (xf - mu) * lax.rsqrt(var + LN_EPS) * g.astype(jnp.float32) + b.astype(jnp.float32)
    return y.astype(x.dtype)


def rope(x, pos):
    d = x.shape[-1]
    half = d // 2
    inv = ROPE_THETA ** (-jnp.arange(half, dtype=jnp.float32) / half)
    ang = pos.astype(jnp.float32)[:, None] * inv[None, :]
    cos, sin = jnp.cos(ang)[None, :, None, :], jnp.sin(ang)[None, :, None, :]
    xf = x.astype(jnp.float32)
    x1, x2 = xf[..., :half], xf[..., half:]
    return jnp.concatenate([x1 * cos - x2 * sin, x1 * sin + x2 * cos], -1).astype(x.dtype)


def masked_softmax(s, mask):
    s = jnp.where(mask, s.astype(jnp.float32), NEG_INF)
    return jnp.where(mask, jax.nn.softmax(s, axis=-1), 0.0)


def split_cols(z):
    out, start = [], 0
    for size in COL_SIZES:
        out.append(z[..., start:start + size])
        start += size
    return out


def gather_pages(cache, layer, page_table):
    b, n_pages = page_table.shape
    rows = cache[layer, page_table]
    return rows.reshape((b, n_pages * PAGE_SIZE) + cache.shape[3:])


def mixer_project(x, pos, w_in):
    b, t, _ = x.shape
    q_n, kc, vc, ks, vs, kw, vw, g_n, q_d, k_d, v_d, g_m = split_cols(x @ w_in)
    kvh = lambda a: a.reshape(b, t, G_NSA, DH_NSA)
    q_n = rope(q_n.reshape(b, t, H_NSA, DH_NSA), pos).reshape(b, t, G_NSA, HPG_NSA, DH_NSA)
    kc, ks, kw = rope(kvh(kc), pos), rope(kvh(ks), pos), rope(kvh(kw), pos)
    vc, vs, vw = kvh(vc), kvh(vs), kvh(vw)
    g_n = jax.nn.sigmoid(g_n.reshape(b, t, H_NSA, 3))
    q_d = q_d.reshape(b, t, H_DIFF, 2, DH_DIFF)
    k_d = k_d.reshape(b, t, H_DIFF, 2, DH_DIFF)
    q1, q2 = rope(q_d[:, :, :, 0], pos), rope(q_d[:, :, :, 1], pos)
    k_d = jnp.concatenate([rope(k_d[:, :, :, 0], pos), rope(k_d[:, :, :, 1], pos)], -1)
    v_d = v_d.reshape(b, t, H_DIFF, DV_DIFF)
    g_m = jax.nn.sigmoid(g_m.reshape(b, t, 2, D_MODEL))
    return (q_n, kc, vc, ks, vs, kw, vw, g_n, q1, q2, k_d, v_d, g_m)


def compress(rows, w1, pe, w2):
    b, t = rows.shape[:2]
    nc = (t - L_CMP) // STRIDE_CMP + 1
    r = rows[:, :STRIDE_CMP * (nc + 1)].reshape(b, nc + 1, STRIDE_CMP, G_NSA, DH_NSA)
    first = jnp.einsum('bnlgd,lde->bnge', r, w1[:STRIDE_CMP])
    second = jnp.einsum('bnlgd,lde->bnge', r, w1[STRIDE_CMP:])
    pe_term = jnp.einsum('ld,lde->e', pe, w1)
    hid = jax.nn.gelu(first[:, :-1] + second[:, 1:] + pe_term, approximate=False)
    return jnp.einsum('bnge,ed->bngd', hid, w2)


def nsa_attend(q, qpos, kc, vc, sel_fn, kw, vw, wpos, gates, n_slc):
    b, tq = q.shape[:2]
    nc = kc.shape[1]
    scale = DH_NSA ** -0.5
    t = qpos[:, None]
    cend = jnp.arange(nc, dtype=jnp.int32) * STRIDE_CMP + (L_CMP - 1)
    s = jnp.einsum('btghd,bngd->btghn', q, kc).astype(jnp.float32) * scale
    p_cmp = masked_softmax(s, (cend[None, :] <= t)[None, :, None, None, :])
    o_cmp = jnp.einsum('btghn,bngd->btghd', p_cmp.astype(vc.dtype), vc)
    ratio = L_SLC // STRIDE_CMP
    imp = jnp.pad(p_cmp.sum(axis=3), ((0, 0), (0, 0), (0, 0), (0, n_slc * ratio - nc)))
    imp = imp.reshape(b, tq, G_NSA, n_slc, ratio).sum(-1)
    blk = jnp.arange(n_slc, dtype=jnp.int32)[None, :]
    cur = (qpos // L_SLC)[:, None]
    forced = (blk == 0) | (blk == cur) | (blk == cur - 1)
    future = blk * L_SLC > t
    score = jnp.where(future[None, :, None, :], -SEL_BONUS,
                      imp + jnp.where(forced, SEL_BONUS, 0.0)[None, :, None, :])
    vals, idx = lax.top_k(score, min(N_SEL, n_slc))
    valid = vals > -0.5 * SEL_BONUS
    spos = idx[..., None] * L_SLC + jnp.arange(L_SLC, dtype=jnp.int32)
    ks, vs = sel_fn(spos)
    smask = (valid[..., None] & (spos <= qpos[None, :, None, None, None])).reshape(b, tq, G_NSA, -1)
    ks = ks.reshape(b, tq, G_NSA, -1, DH_NSA)
    vs = vs.reshape(b, tq, G_NSA, -1, DH_NSA)
    s = jnp.einsum('btghd,btgkd->btghk', q, ks).astype(jnp.float32) * scale
    p = masked_softmax(s, smask[:, :, :, None, :])
    o_slc = jnp.einsum('btghk,btgkd->btghd', p.astype(vs.dtype), vs)
    dist = t - wpos[None, :]
    wmask = (dist >= 0) & (dist <= WINDOW) & (wpos[None, :] >= 0)
    s = jnp.einsum('btghd,bsgd->btghs', q, kw).astype(jnp.float32) * scale
    p = masked_softmax(s, wmask[None, :, None, None, :])
    o_win = jnp.einsum('btghs,bsgd->btghd', p.astype(vw.dtype), vw)
    g = gates.reshape(b, tq, G_NSA, HPG_NSA, 3)
    o = g[..., 0:1] * o_cmp + g[..., 1:2] * o_slc + g[..., 2:3] * o_win
    return o.reshape(b, tq, H_NSA * DH_NSA)


def nsa_prompt(q, kc_rows, vc_rows, ks_rows, vs_rows, kw_rows, vw_rows, gates, cmp_k, cmp_v):
    b, t = q.shape[:2]
    kc, vc = compress(kc_rows, *cmp_k), compress(vc_rows, *cmp_v)
    n_slc = -(-t // L_SLC)
    bidx = jnp.arange(b)[:, None, None, None, None]
    gidx = jnp.arange(G_NSA)[None, None, :, None, None]

    def sel_fn(spos):
        p = jnp.clip(spos, 0, t - 1)
        return ks_rows[bidx, p, gidx], vs_rows[bidx, p, gidx]

    pad = ((0, 0), (WINDOW, 0), (0, 0), (0, 0))
    kw_pad, vw_pad = jnp.pad(kw_rows, pad), jnp.pad(vw_rows, pad)
    n_blk = t // Q_BLOCK
    q_blk = q.reshape(b, n_blk, Q_BLOCK, G_NSA, HPG_NSA, DH_NSA).swapaxes(0, 1)
    g_blk = gates.reshape(b, n_blk, Q_BLOCK, H_NSA, 3).swapaxes(0, 1)

    def one_block(args):
        i, qb, gb = args
        q0 = i * Q_BLOCK
        qpos = q0 + jnp.arange(Q_BLOCK, dtype=jnp.int32)
        wpos = q0 - WINDOW + jnp.arange(WINDOW + Q_BLOCK, dtype=jnp.int32)
        kw = lax.dynamic_slice_in_dim(kw_pad, q0, WINDOW + Q_BLOCK, axis=1)
        vw = lax.dynamic_slice_in_dim(vw_pad, q0, WINDOW + Q_BLOCK, axis=1)
        return nsa_attend(qb, qpos, kc, vc, sel_fn, kw, vw, wpos, gb, n_slc)

    out = lax.map(one_block, (jnp.arange(n_blk, dtype=jnp.int32), q_blk, g_blk))
    return out.swapaxes(0, 1).reshape(b, t, H_NSA * DH_NSA)


def nsa_sample(q, kc_new, vc_new, ks_new, vs_new, kw_new, vw_new, gates, cache_cmp_k, cache_cmp_v,
               cache_slc_k, cache_slc_v, win_k, win_v, page_table, layer, cmp_k, cmp_v):
    b, tn = q.shape[:2]
    past_len = page_table.shape[1] * PAGE_SIZE
    t_total = past_len + tn
    kc = compress(jnp.concatenate([gather_pages(cache_cmp_k, layer, page_table), kc_new], 1), *cmp_k)
    vc = compress(jnp.concatenate([gather_pages(cache_cmp_v, layer, page_table), vc_new], 1), *cmp_v)
    bidx = jnp.arange(b)[:, None, None, None, None]
    gidx = jnp.arange(G_NSA)[None, None, :, None, None]

    def sel_fn(spos):
        in_past = (spos < past_len)[..., None]
        pp = jnp.clip(spos, 0, past_len - 1)
        page = page_table[bidx, pp // PAGE_SIZE]
        off = pp % PAGE_SIZE
        pn = jnp.clip(spos - past_len, 0, tn - 1)
        k = jnp.where(in_past, cache_slc_k[layer, page, off, gidx], ks_new[bidx, pn, gidx])
        v = jnp.where(in_past, cache_slc_v[layer, page, off, gidx], vs_new[bidx, pn, gidx])
        return k, v

    wbuf = win_k.shape[1]
    kw = jnp.concatenate([win_k, kw_new], 1)
    vw = jnp.concatenate([win_v, vw_new], 1)
    wpos = past_len - wbuf + jnp.arange(wbuf + tn, dtype=jnp.int32)
    qpos = past_len + jnp.arange(tn, dtype=jnp.int32)
    o = nsa_attend(q, qpos, kc, vc, sel_fn, kw, vw, wpos, gates, -(-t_total // L_SLC))
    keep = min(WINDOW, t_total)
    return o, kw[:, wbuf + tn - keep:], vw[:, wbuf + tn - keep:]


def diff_attend(q1, q2, k, v, qpos, kpos, lam, subln_g, lam_init):
    b, tq = q1.shape[:2]
    scale = DH_DIFF ** -0.5
    mask = (kpos[None, :] <= qpos[:, None])[None, None]
    k1, k2 = k[..., :DH_DIFF], k[..., DH_DIFF:]
    a1 = masked_softmax(jnp.einsum('bthd,bshd->bhts', q1, k1).astype(jnp.float32) * scale, mask)
    a2 = masked_softmax(jnp.einsum('bthd,bshd->bhts', q2, k2).astype(jnp.float32) * scale, mask)
    o = jnp.einsum('bhts,bshd->bthd', (a1 - lam * a2).astype(v.dtype), v).astype(jnp.float32)
    o = o * lax.rsqrt(jnp.mean(jnp.square(o), -1, keepdims=True) + LN_EPS) * subln_g.astype(jnp.float32)
    return (o * (1.0 - lam_init)).astype(v.dtype).reshape(b, tq, H_DIFF * DV_DIFF)


def diff_prompt(q1, q2, k, v, lam, subln_g, lam_init):
    b, t = q1.shape[:2]
    n_blk = t // Q_BLOCK
    kpos = jnp.arange(t, dtype=jnp.int32)
    qb1 = q1.reshape(b, n_blk, Q_BLOCK, H_DIFF, DH_DIFF).swapaxes(0, 1)
    qb2 = q2.reshape(b, n_blk, Q_BLOCK, H_DIFF, DH_DIFF).swapaxes(0, 1)

    def one_block(args):
        i, a, c = args
        qpos = i * Q_BLOCK + jnp.arange(Q_BLOCK, dtype=jnp.int32)
        return diff_attend(a, c, k, v, qpos, kpos, lam, subln_g, lam_init)

    out = lax.map(one_block, (jnp.arange(n_blk, dtype=jnp.int32), qb1, qb2))
    return out.swapaxes(0, 1).reshape(b, t, H_DIFF * DV_DIFF)


def mixer_output(o_nsa, o_diff, g_m, w_br_nsa, w_br_diff, w_out):
    m = g_m[:, :, 0] * (o_nsa @ w_br_nsa) + g_m[:, :, 1] * (o_diff @ w_br_diff)
    return m @ w_out


def peer_ffn(xf, wq, subkeys, u, v):
    n = xf.shape[0]
    q = (xf @ wq).reshape(n, PEER_HEADS, 2, PEER_DQ // 2)
    s = jnp.einsum('nhpc,hpkc->nhpk', q, subkeys).astype(jnp.float32)
    s1, i1 = lax.top_k(s[:, :, 0], PEER_TOPK)
    s2, i2 = lax.top_k(s[:, :, 1], PEER_TOPK)
    cand = (s1[..., :, None] + s2[..., None, :]).reshape(n, PEER_HEADS, PEER_TOPK * PEER_TOPK)
    cid = (i1[..., :, None] * N_KEYS + i2[..., None, :]).reshape(n, PEER_HEADS, PEER_TOPK * PEER_TOPK)
    top, sel = lax.top_k(cand, PEER_TOPK)
    eid = jnp.take_along_axis(cid, sel, axis=-1)
    g = jax.nn.softmax(top, axis=-1)
    act = jax.nn.gelu(jnp.einsum('nd,nhkd->nhk', xf, u[eid]), approximate=False)
    return jnp.einsum('nhk,nhkd->nd', (g * act.astype(jnp.float32)).astype(xf.dtype), v[eid])


_BF16 = jnp.bfloat16
_F32 = jnp.float32
_VMEM_LIMIT = 56 * 1024 * 1024


def _dot_nt(a, b):
    return lax.dot_general(a, b, (((1,), (1,)), ((), ())), preferred_element_type=_F32)


def _dot(a, b):
    return jnp.dot(a, b, preferred_element_type=_F32)


def _dot_exact01(a, m01):
    hi = a.astype(_BF16)
    r1 = a - hi.astype(_F32)
    mid = r1.astype(_BF16)
    lo = (r1 - mid.astype(_F32)).astype(_BF16)
    return _dot(hi, m01) + _dot(mid, m01) + _dot(lo, m01)


def _gelu_erf(x):
    return 0.5 * x * (1.0 + lax.erf(x * (2.0 ** -0.5)))


def _softmax_rows(s, mask):
    s = jnp.where(mask, s, NEG_INF)
    m = jnp.max(s, axis=-1, keepdims=True)
    e = jnp.where(mask, jnp.exp(s - m), 0.0)
    l = jnp.sum(e, axis=-1, keepdims=True)
    return e / jnp.where(l > 0.0, l, 1.0)


def _compress_kernel(r_ref, w1a_ref, w1b_ref, pe_ref, w1_ref, w2_ref, o_ref):
    r = r_ref[...]
    first = _dot(r, w1a_ref[...])
    second = _dot(r, w1b_ref[...])
    nb = first.shape[0]
    second_next = jnp.concatenate([second[1:], jnp.zeros((1, CMP_HID), _F32)], axis=0)
    pe_term = _dot(pe_ref[...], w1_ref[...])[0:1]
    hid = _gelu_erf(first + second_next + pe_term)
    o_ref[...] = _dot(hid.astype(_BF16), w2_ref[...])


def compress_pallas(rows, w1, pe, w2):
    t = rows.shape[0]
    nb = t // STRIDE_CMP
    r = rows.reshape(nb, STRIDE_CMP, G_NSA, DH_NSA).transpose(2, 0, 1, 3)
    r = r.reshape(G_NSA, nb, STRIDE_CMP * DH_NSA).astype(_BF16)
    w1f = w1.reshape(L_CMP * DH_NSA, CMP_HID).astype(_BF16)
    half = STRIDE_CMP * DH_NSA
    pe8 = jnp.broadcast_to(pe.reshape(1, L_CMP * DH_NSA), (8, L_CMP * DH_NSA)).astype(_BF16)
    full = lambda shape: pl.BlockSpec(shape, lambda g: (0,) * len(shape))
    return pl.pallas_call(
        _compress_kernel,
        grid=(G_NSA,),
        in_specs=[pl.BlockSpec((None, nb, half), lambda g: (g, 0, 0)),
                  full((half, CMP_HID)), full((half, CMP_HID)),
                  full((8, 2 * half)), full((2 * half, CMP_HID)), full((CMP_HID, DH_NSA))],
        out_specs=pl.BlockSpec((None, nb, DH_NSA), lambda g: (g, 0, 0)),
        out_shape=jax.ShapeDtypeStruct((G_NSA, nb, DH_NSA), _F32),
        name="nsa_compress",
    )(r, w1f[:half], w1f[half:], pe8, w1f, w2.astype(_BF16))


_NSA_TQ = 128
_NSA_TK = 512


def _nsa_prompt_kernel(q_ref, kc_ref, vc_ref, ks_ref, vs_ref, kw_ref, vw_ref, g_ref, o_ref, st_ref,
                       *, nc, t_total):
    tq, tk, hh = _NSA_TQ, _NSA_TK, HPG_NSA
    q0 = pl.program_id(1) * tq
    qt = q_ref[...]
    q = jnp.concatenate([qt[:, h * DH_NSA:(h + 1) * DH_NSA] for h in range(hh)], axis=0)
    q = (q * (DH_NSA ** -0.5)).astype(_BF16)
    tpos = q0 + lax.broadcasted_iota(jnp.int32, (tq, 1), 0)

    ncp = kc_ref.shape[0]
    s = _dot_nt(q, kc_ref[...]).reshape(hh, tq, ncp)
    n_idx = lax.broadcasted_iota(jnp.int32, (tq, ncp), 1)
    cmask = ((n_idx * STRIDE_CMP + (L_CMP - 1)) <= tpos) & (n_idx < nc)
    p = _softmax_rows(s, cmask[None])
    o_cmp = _dot(p.reshape(hh * tq, ncp).astype(_BF16), vc_ref[...])
    ratio = L_SLC // STRIDE_CMP
    n_slc = ncp // ratio
    pool = (lax.broadcasted_iota(jnp.int32, (ncp, n_slc), 0) // ratio
            == lax.broadcasted_iota(jnp.int32, (ncp, n_slc), 1)).astype(_BF16)
    imp = _dot_exact01(jnp.sum(p, axis=0), pool)

    blk = lax.broadcasted_iota(jnp.int32, (tq, n_slc), 1)
    cur = tpos // L_SLC
    forced = (blk == 0) | (blk == cur) | (blk == cur - 1)
    future = blk * L_SLC > tpos
    score = jnp.where(future, -SEL_BONUS, imp + jnp.where(forced, SEL_BONUS, 0.0))
    st_ref[...] = score.T
    st = st_ref[...]
    b_idx = lax.broadcasted_iota(jnp.int32, (n_slc, tq), 0)

    def rank_body(bp, cnt):
        row = st_ref[pl.ds(bp, 1), :]
        ahead = (row > st) | ((row == st) & (b_idx > bp))
        return cnt + jnp.where(ahead, 1.0, 0.0)

    cnt = lax.fori_loop(0, n_slc, rank_body, jnp.zeros((n_slc, tq), _F32), unroll=8)
    sel_t = (cnt < float(N_SEL)) & (st > -0.5 * SEL_BONUS)
    sel = sel_t.astype(_F32).T.astype(_BF16)

    def slc_body(c, carry):
        m, l, acc = carry
        k0 = pl.multiple_of(c * tk, tk)
        kk = ks_ref[pl.ds(k0, tk), :]
        vv = vs_ref[pl.ds(k0, tk), :]
        sc = _dot_nt(q, kk).reshape(hh, tq, tk)
        kpos = k0 + lax.broadcasted_iota(jnp.int32, (1, tk), 1)
        expand = ((k0 + lax.broadcasted_iota(jnp.int32, (n_slc, tk), 1)) // L_SLC
                  == lax.broadcasted_iota(jnp.int32, (n_slc, tk), 0)).astype(_BF16)
        mk = (_dot(sel, expand) > 0.5) & (kpos <= tpos)
        mk = mk[None]
        sc = jnp.where(mk, sc, NEG_INF)
        m_new = jnp.maximum(m, jnp.max(sc, axis=-1, keepdims=True))
        alpha = jnp.exp(m - m_new)
        pp = jnp.where(mk, jnp.exp(sc - m_new), 0.0)
        l = alpha * l + jnp.sum(pp, axis=-1, keepdims=True)
        acc = acc * alpha.reshape(hh * tq, 1) + _dot(pp.reshape(hh * tq, tk).astype(_BF16), vv)
        return m_new, l, acc

    n_chunks = (q0 + tq + tk - 1) // tk
    m0 = jnp.full((hh, tq, 1), NEG_INF, _F32)
    l0 = jnp.zeros((hh, tq, 1), _F32)
    a0 = jnp.zeros((hh * tq, DH_NSA), _F32)
    _, l_s, acc_s = lax.fori_loop(0, n_chunks, slc_body, (m0, l0, a0))
    o_slc = acc_s / jnp.where(l_s > 0.0, l_s, 1.0).reshape(hh * tq, 1)

    wlen = WINDOW + tq
    w0 = pl.multiple_of(jnp.maximum(q0 - WINDOW, 0), tq)
    sw = _dot_nt(q, kw_ref[pl.ds(w0, wlen), :]).reshape(hh, tq, wlen)
    dist = tpos - (w0 + lax.broadcasted_iota(jnp.int32, (1, wlen), 1))
    wmask = (dist >= 0) & (dist <= WINDOW)
    pw = _softmax_rows(sw, wmask[None])
    o_win = _dot(pw.reshape(hh * tq, wlen).astype(_BF16), vw_ref[pl.ds(w0, wlen), :])

    g = g_ref[...]
    outs = []
    for h in range(hh):
        rows = slice(h * tq, (h + 1) * tq)
        outs.append(g[:, 3 * h:3 * h + 1] * o_cmp[rows] + g[:, 3 * h + 1:3 * h + 2] * o_slc[rows]
                    + g[:, 3 * h + 2:3 * h + 3] * o_win[rows])
    o_ref[...] = jnp.concatenate(outs, axis=1)


def nsa_prompt_pallas(q, kc, vc, ks_rows, vs_rows, kw_rows, vw_rows, gates, nc):
    t = q.shape[0]
    ncp = kc.shape[1]
    tq = _NSA_TQ
    assert t % _NSA_TK == 0 and t >= WINDOW + tq and ncp * STRIDE_CMP == t
    per_group = lambda a: a.reshape(t, G_NSA, DH_NSA).transpose(1, 0, 2).astype(_BF16)
    g3 = gates.reshape(t, G_NSA, HPG_NSA * 3).transpose(1, 0, 2)
    hw = HPG_NSA * DH_NSA
    kv_spec = pl.BlockSpec((None, t, DH_NSA), lambda g, i: (g, 0, 0))
    c_spec = pl.BlockSpec((None, ncp, DH_NSA), lambda g, i: (g, 0, 0))
    return pl.pallas_call(
        functools.partial(_nsa_prompt_kernel, nc=nc, t_total=t),
        grid=(G_NSA, t // tq),
        in_specs=[pl.BlockSpec((tq, hw), lambda g, i: (i, g)),
                  c_spec, c_spec, kv_spec, kv_spec, kv_spec, kv_spec,
                  pl.BlockSpec((None, tq, HPG_NSA * 3), lambda g, i: (g, i, 0))],
        out_specs=pl.BlockSpec((tq, hw), lambda g, i: (i, g)),
        out_shape=jax.ShapeDtypeStruct((t, H_NSA * DH_NSA), _F32),
        scratch_shapes=[pltpu.VMEM((ncp // (L_SLC // STRIDE_CMP), tq), _F32)],
        compiler_params=pltpu.CompilerParams(
            dimension_semantics=("arbitrary", "arbitrary"), vmem_limit_bytes=_VMEM_LIMIT),
        name="nsa_prompt",
    )(q, kc.astype(_BF16), vc.astype(_BF16), per_group(ks_rows), per_group(vs_rows),
      per_group(kw_rows), per_group(vw_rows), g3)


_DIFF_TQ = 256
_DIFF_TK = 512


def _diff_prompt_kernel(lam_ref, q_ref, k_ref, v_ref, gam_ref, o_ref, *, out_scale):
    tq, tk = _DIFF_TQ, _DIFF_TK
    q0 = pl.program_id(1) * tq
    qt = (q_ref[...] * (DH_DIFF ** -0.5)).astype(_BF16)
    q1, q2 = qt[:, :DH_DIFF], qt[:, DH_DIFF:]
    tpos = q0 + lax.broadcasted_iota(jnp.int32, (tq, 1), 0)

    def body(c, carry):
        m1, l1, a1, m2, l2, a2 = carry
        k0 = pl.multiple_of(c * tk, tk)
        kk = k_ref[pl.ds(k0, tk), :]
        vv = v_ref[pl.ds(k0, tk), :]
        mk = (k0 + lax.broadcasted_iota(jnp.int32, (1, tk), 1)) <= tpos

        def stream(qh, kh, m, l, a):
            sc = jnp.where(mk, _dot_nt(qh, kh), NEG_INF)
            m_new = jnp.maximum(m, jnp.max(sc, axis=-1, keepdims=True))
            alpha = jnp.exp(m - m_new)
            pp = jnp.where(mk, jnp.exp(sc - m_new), 0.0)
            l = alpha * l + jnp.sum(pp, axis=-1, keepdims=True)
            a = alpha * a + _dot(pp.astype(_BF16), vv)
            return m_new, l, a

        m1, l1, a1 = stream(q1, kk[:, :DH_DIFF], m1, l1, a1)
        m2, l2, a2 = stream(q2, kk[:, DH_DIFF:], m2, l2, a2)
        return m1, l1, a1, m2, l2, a2

    n_chunks = (q0 + tq + tk - 1) // tk
    mi = jnp.full((tq, 1), NEG_INF, _F32)
    li = jnp.zeros((tq, 1), _F32)
    ai = jnp.zeros((tq, DV_DIFF), _F32)
    _, l1, a1, _, l2, a2 = lax.fori_loop(0, n_chunks, body, (mi, li, ai, mi, li, ai))
    lam = lam_ref[0:1, 0:1]
    o = a1 / l1 - lam * (a2 / l2)
    o = o * lax.rsqrt(jnp.mean(o * o, axis=-1, keepdims=True) + LN_EPS) * gam_ref[...]
    o_ref[...] = o * out_scale


def diff_prompt_pallas(q, k, v, lam, subln_g, lam_init):
    t = q.shape[0]
    tq = _DIFF_TQ
    assert t % _DIFF_TK == 0
    lam_row = jnp.full((8, 128), lam, _F32)
    hw = 2 * DH_DIFF
    return pl.pallas_call(
        functools.partial(_diff_prompt_kernel, out_scale=1.0 - lam_init),
        grid=(H_DIFF, t // tq),
        in_specs=[pl.BlockSpec((8, 128), lambda h, i: (0, 0)),
                  pl.BlockSpec((tq, hw), lambda h, i: (i, h)),
                  pl.BlockSpec((t, hw), lambda h, i: (0, h)),
                  pl.BlockSpec((t, DV_DIFF), lambda h, i: (0, h)),
                  pl.BlockSpec((1, DV_DIFF), lambda h, i: (0, 0))],
        out_specs=pl.BlockSpec((tq, DV_DIFF), lambda h, i: (i, h)),
        out_shape=jax.ShapeDtypeStruct((t, H_DIFF * DV_DIFF), _F32),
        compiler_params=pltpu.CompilerParams(
            dimension_semantics=("arbitrary", "arbitrary"), vmem_limit_bytes=_VMEM_LIMIT),
        name="diff_prompt",
    )(lam_row, q, k.astype(_BF16), v.astype(_BF16), subln_g.reshape(1, DV_DIFF))


_SD_PAGES = 8
_SD_STREAMS = 2 * H_DIFF


def _diff_sample_kernel(pt_ref, lam_ref, qm_ref, kn_ref, vn_ref, gam_ref, *refs, out_scale):
    k_refs, v_refs = refs[:_SD_PAGES], refs[_SD_PAGES:2 * _SD_PAGES]
    o_ref, m_ref, l_ref, acc_ref = refs[2 * _SD_PAGES:]
    c = pl.program_id(1)
    qm = qm_ref[...]

    @pl.when(c == 0)
    def _():
        kn = kn_ref[...].astype(_BF16)
        m_ref[...] = _dot_nt(qm, kn)[:, 0:1]
        l_ref[...] = jnp.ones(l_ref.shape, _F32)
        vn = vn_ref[...].astype(_BF16).astype(_F32)
        acc_ref[...] = jnp.broadcast_to(vn[0:1, :], acc_ref.shape)

    kk = jnp.concatenate([r[...].astype(_BF16) for r in k_refs], axis=0)
    vv = jnp.concatenate([r[...].astype(_BF16) for r in v_refs], axis=0)
    sc = _dot_nt(qm, kk)
    m_old = m_ref[...]
    m_new = jnp.maximum(m_old, jnp.max(sc, axis=-1, keepdims=True))
    alpha = jnp.exp(m_old - m_new)
    pp = jnp.exp(sc - m_new)
    m_ref[...] = m_new
    l_ref[...] = alpha * l_ref[...] + jnp.sum(pp, axis=-1, keepdims=True)
    acc_ref[...] = alpha * acc_ref[...] + _dot(pp.astype(_BF16), vv)

    @pl.when(c == pl.num_programs(1) - 1)
    def _():
        lam = lam_ref[0:1, 0:1]
        on = acc_ref[...] / l_ref[...]
        outs = []
        for h in range(H_DIFF):
            lanes = slice(h * DV_DIFF, (h + 1) * DV_DIFF)
            o = on[2 * h:2 * h + 1, lanes] - lam * on[2 * h + 1:2 * h + 2, lanes]
            o = o * lax.rsqrt(jnp.mean(o * o, axis=-1, keepdims=True) + LN_EPS) * gam_ref[...]
            outs.append(o * out_scale)
        o_ref[...] = jnp.concatenate(outs, axis=1)


def diff_sample_pallas(q1, q2, k_new, v_new, cache_k, cache_v, page_table, lam, subln_g, lam_init):
    b, n_pages = page_table.shape
    hw = H_DIFF * 2 * DH_DIFF
    assert n_pages % _SD_PAGES == 0
    qs = jnp.stack([q1, q2], axis=2) * (DH_DIFF ** -0.5)
    eye_h = jnp.eye(H_DIFF, dtype=_F32)
    eye_j = jnp.eye(2, dtype=_F32)
    qm = jnp.einsum('bhjd,hg,ji->bhjgid', qs, eye_h, eye_j).reshape(b, _SD_STREAMS, hw).astype(_BF16)
    pad8 = lambda a: jnp.pad(a[:, None, :], ((0, 0), (0, 7), (0, 0)))
    lam_row = jnp.full((8, 128), lam, _F32)
    page_spec = lambda j: pl.BlockSpec((None, PAGE_SIZE, hw),
                                       lambda bb, c, pt: (pt[bb, c * _SD_PAGES + j], 0, 0))
    per_b = lambda rows: pl.BlockSpec((None, rows, hw), lambda bb, c, pt: (bb, 0, 0))
    grid_spec = pltpu.PrefetchScalarGridSpec(
        num_scalar_prefetch=1,
        grid=(b, n_pages // _SD_PAGES),
        in_specs=[pl.BlockSpec((8, 128), lambda bb, c, pt: (0, 0)),
                  per_b(_SD_STREAMS), per_b(8), per_b(8),
                  pl.BlockSpec((1, DV_DIFF), lambda bb, c, pt: (0, 0))]
                 + [page_spec(j) for j in range(_SD_PAGES)] * 2,
        out_specs=pl.BlockSpec((None, 1, hw), lambda bb, c, pt: (bb, 0, 0)),
        scratch_shapes=[pltpu.VMEM((_SD_STREAMS, 1), _F32), pltpu.VMEM((_SD_STREAMS, 1), _F32),
                        pltpu.VMEM((_SD_STREAMS, hw), _F32)],
    )
    out = pl.pallas_call(
        functools.partial(_diff_sample_kernel, out_scale=1.0 - lam_init),
        grid_spec=grid_spec,
        out_shape=jax.ShapeDtypeStruct((b, 1, hw), _F32),
        compiler_params=pltpu.CompilerParams(
            dimension_semantics=("arbitrary", "arbitrary"), vmem_limit_bytes=_VMEM_LIMIT),
        name="diff_sample",
    )(page_table, lam_row, qm, pad8(k_new), pad8(v_new), subln_g.reshape(1, DV_DIFF),
      *([cache_k] * _SD_PAGES), *([cache_v] * _SD_PAGES))
    return out.reshape(b, hw)


_PEER_PAIRS = tuple((x, y) for x in range(PEER_TOPK) for y in range(PEER_TOPK)
                    if (x + 1) * (y + 1) <= PEER_TOPK)
_PEER_NPAIR_PAD = -(-len(_PEER_PAIRS) // 8) * 8
_PEER_HALF = PEER_DQ // 2


def _distinct_top(x):
    vals, cnts = [], []
    for _ in range(PEER_TOPK):
        m = jnp.max(x, axis=0, keepdims=True)
        eq = x == m
        vals.append(m)
        cnts.append(jnp.sum(jnp.where(eq, 1.0, 0.0), axis=0, keepdims=True))
        x = jnp.where(eq, NEG_INF, x)
    return vals, cnts


def _peer_a_kernel(h_ref, wq_ref, sk_ref, s1_ref, e1_ref, s2_ref, e2_ref, tau_ref, c_ref, w_ref):
    tn = h_ref.shape[0]
    q = _dot(h_ref[...], wq_ref[...]).astype(_BF16)
    cols = lambda hh, p: q[:, (2 * hh + p) * _PEER_HALF:(2 * hh + p + 1) * _PEER_HALF]
    s1 = jnp.concatenate([_dot_nt(sk_ref[2 * hh], cols(hh, 0)) for hh in range(PEER_HEADS)], axis=1)
    s2 = jnp.concatenate([_dot_nt(sk_ref[2 * hh + 1], cols(hh, 1)) for hh in range(PEER_HEADS)], axis=1)
    a, ca = _distinct_top(s1)
    b, cb = _distinct_top(s2)
    c_ref[...] = jnp.full(c_ref.shape, NEG_INF, _F32)
    w_ref[...] = jnp.zeros(w_ref.shape, _F32)
    for k, (x, y) in enumerate(_PEER_PAIRS):
        c_ref[k:k + 1, :] = a[x] + b[y]
        w_ref[k:k + 1, :] = ca[x] * cb[y]
    cand, wgt = c_ref[...], w_ref[...]
    tau = jnp.full(a[0].shape, NEG_INF, _F32)
    for k in range(len(_PEER_PAIRS)):
        cu = c_ref[k:k + 1, :]
        n_ge = jnp.sum(jnp.where(cand >= cu, wgt, 0.0), axis=0, keepdims=True)
        tau = jnp.maximum(tau, jnp.where(n_ge >= float(PEER_TOPK), cu, NEG_INF))
    top = a[0] + b[0]
    z = jnp.sum(jnp.where(cand >= tau, wgt * jnp.exp(cand - top), 0.0), axis=0, keepdims=True)
    e1 = jnp.exp(s1 - a[0])
    e2 = jnp.exp(s2 - b[0]) / z
    for hh in range(PEER_HEADS):
        lanes = slice(hh * tn, (hh + 1) * tn)
        s1_ref[hh] = s1[:, lanes]
        e1_ref[hh] = e1[:, lanes]
        s2_ref[hh] = s2[:, lanes]
        e2_ref[hh] = e2[:, lanes]
        tau_ref[hh:hh + 1, :] = tau[:, lanes]


def _peer_b_kernel(h_ref, s1_ref, e1_ref, s2_ref, e2_ref, tau_ref, u_ref, v_ref, o_ref):
    tn = h_ref.shape[0]
    c = pl.program_id(1)
    n_sub = u_ref.shape[0] // N_KEYS
    parts = []
    for ii in range(n_sub):
        i = c * n_sub + ii
        g_t = jnp.zeros((N_KEYS, tn), _F32)
        for hh in range(PEER_HEADS):
            pair = s2_ref[hh] + s1_ref[hh, pl.ds(i, 1), :]
            g_t = g_t + jnp.where(pair >= tau_ref[hh:hh + 1, :], e2_ref[hh], 0.0) * e1_ref[hh, pl.ds(i, 1), :]
        parts.append(g_t)
    g_t = jnp.concatenate(parts, axis=0)
    a_t = _dot_nt(u_ref[...], h_ref[...])
    w_t = (g_t * _gelu_erf(a_t)).astype(_BF16)
    contrib = lax.dot_general(w_t, v_ref[...], (((0,), (0,)), ((), ())), preferred_element_type=_F32)

    @pl.when(c == 0)
    def _():
        o_ref[...] = contrib

    @pl.when(c > 0)
    def _():
        o_ref[...] += contrib


_PEER_E_CHUNK = 512
_PEER_TN = 512


def peer_ffn_pallas(h, wq, subkeys, u, v, *, tn):
    n, d = h.shape
    assert n % tn == 0
    hb = h.astype(_BF16)
    r = PEER_HEADS * tn
    stat = jax.ShapeDtypeStruct((PEER_HEADS, N_KEYS, n), _F32)
    stat_spec = pl.BlockSpec((PEER_HEADS, N_KEYS, tn), lambda i: (0, 0, i))
    s1, e1, s2, e2, tau = pl.pallas_call(
        _peer_a_kernel,
        grid=(n // tn,),
        in_specs=[pl.BlockSpec((tn, d), lambda i: (i, 0)),
                  pl.BlockSpec(wq.shape, lambda i: (0, 0)),
                  pl.BlockSpec(subkeys.shape, lambda i: (0, 0, 0))],
        out_specs=[stat_spec, stat_spec, stat_spec, stat_spec,
                   pl.BlockSpec((PEER_HEADS, tn), lambda i: (0, i))],
        out_shape=[stat, stat, stat, stat, jax.ShapeDtypeStruct((PEER_HEADS, n), _F32)],
        scratch_shapes=[pltpu.VMEM((_PEER_NPAIR_PAD, r), _F32), pltpu.VMEM((_PEER_NPAIR_PAD, r), _F32)],
        compiler_params=pltpu.CompilerParams(
            dimension_semantics=("arbitrary",), vmem_limit_bytes=_VMEM_LIMIT),
        name="peer_scores",
    )(hb, wq, subkeys)
    e = _PEER_E_CHUNK
    stat_spec_b = pl.BlockSpec((PEER_HEADS, N_KEYS, tn), lambda i, c: (0, 0, i))
    return pl.pallas_call(
        _peer_b_kernel,
        grid=(n // tn, N_EXPERTS // e),
        in_specs=[pl.BlockSpec((tn, d), lambda i, c: (i, 0)),
                  stat_spec_b, stat_spec_b, stat_spec_b, stat_spec_b,
                  pl.BlockSpec((PEER_HEADS, tn), lambda i, c: (0, i)),
                  pl.BlockSpec((e, d), lambda i, c: (c, 0)),
                  pl.BlockSpec((e, d), lambda i, c: (c, 0))],
        out_specs=pl.BlockSpec((tn, d), lambda i, c: (i, 0)),
        out_shape=jax.ShapeDtypeStruct((n, d), _F32),
        compiler_params=pltpu.CompilerParams(
            dimension_semantics=("arbitrary", "arbitrary"), vmem_limit_bytes=_VMEM_LIMIT),
        name="peer_experts",
    )(hb, s1, e1, s2, e2, tau, u, v)


def _ln_residual_kernel(a_ref, f_ref, g_ref, b_ref, o_ref):
    x = ALPHA * a_ref[...] + f_ref[...]
    mu = jnp.mean(x, axis=-1, keepdims=True)
    xc = x - mu
    var = jnp.mean(xc * xc, axis=-1, keepdims=True)
    o_ref[...] = xc * lax.rsqrt(var + LN_EPS) * g_ref[...] + b_ref[...]


def ln_residual(a, f, g, b, *, tm=256):
    n, d = a.shape
    tm = min(tm, n)
    return pl.pallas_call(
        _ln_residual_kernel,
        grid=(n // tm,),
        in_specs=[pl.BlockSpec((tm, d), lambda i: (i, 0)),
                  pl.BlockSpec((tm, d), lambda i: (i, 0)),
                  pl.BlockSpec((1, d), lambda i: (0, 0)),
                  pl.BlockSpec((1, d), lambda i: (0, 0))],
        out_specs=pl.BlockSpec((tm, d), lambda i: (i, 0)),
        out_shape=jax.ShapeDtypeStruct((n, d), jnp.float32),
        name="ln_residual",
    )(a, f, g.reshape(1, d), b.reshape(1, d))


def prompt_attention(q_n, kc, vc, ks, vs, kw, vw, g_n, q1, q2, kd, vd, cmp_k, cmp_v, lam, subln_g, lam_init):
    t = q_n.shape[1]
    flat = lambda a: a.reshape(t, -1)
    nc = (t - L_CMP) // STRIDE_CMP + 1
    kc_c = compress_pallas(flat(kc), *cmp_k)
    vc_c = compress_pallas(flat(vc), *cmp_v)
    o_n = nsa_prompt_pallas(flat(q_n), kc_c, vc_c, flat(ks), flat(vs), flat(kw), flat(vw), flat(g_n), nc)
    q_d = jnp.concatenate([q1, q2], axis=-1)
    o_d = diff_prompt_pallas(flat(q_d), flat(kd), flat(vd), lam, subln_g, lam_init)
    return o_n[None], o_d[None]


def post_block(x, mix, ln1_g, ln1_b, ln2_g, ln2_b, peer_w):
    b, t, d = x.shape
    n = b * t
    h = ln_residual(x.reshape(n, d), mix.reshape(n, d), ln1_g, ln1_b)
    tn = min(_PEER_TN, -(-n // Q_BLOCK) * Q_BLOCK)
    n_pad = -(-n // tn) * tn
    f = peer_ffn_pallas(jnp.pad(h, ((0, n_pad - n), (0, 0))), *peer_w, tn=tn)[:n]
    return ln_residual(h, f, ln2_g, ln2_b).reshape(b, t, d)


def kernel(x_prompt, x_sample, cache_diff_k, cache_diff_v, cache_nsa_cmp_k, cache_nsa_cmp_v,
           cache_nsa_slc_k, cache_nsa_slc_v, state_nsa_win_k, state_nsa_win_v, page_table,
           w_in, cmp_w1_k, cmp_pe_k, cmp_w2_k, cmp_w1_v, cmp_pe_v, cmp_w2_v,
           lambda_q1, lambda_k1, lambda_q2, lambda_k2, diff_subln_g,
           w_br_nsa, w_br_diff, w_out, ln1_g, ln1_b,
           peer_wq, peer_subkeys, peer_u, peer_v, ln2_g, ln2_b):
    past_len = page_table.shape[1] * PAGE_SIZE
    seq, dec_seq = x_prompt.shape[1], x_sample.shape[1]
    assert x_prompt.shape[0] == 1 and dec_seq == 1 and w_in.shape[0] == 1
    pos_p = jnp.arange(seq, dtype=jnp.int32)
    pos_s = past_len + jnp.arange(dec_seq, dtype=jnp.int32)
    l = 0
    lam_init = 0.8 - 0.6 * math.exp(-0.3 * l)
    lam = (jnp.exp(jnp.sum(lambda_q1[l] * lambda_k1[l]))
           - jnp.exp(jnp.sum(lambda_q2[l] * lambda_k2[l])) + lam_init)
    cmp_k = (cmp_w1_k[l], cmp_pe_k[l], cmp_w2_k[l])
    cmp_v = (cmp_w1_v[l], cmp_pe_v[l], cmp_w2_v[l])
    peer_w = (peer_wq[l].astype(_BF16),
              peer_subkeys[l].reshape(PEER_HEADS * 2, N_KEYS, _PEER_HALF).astype(_BF16),
              peer_u[l].astype(_BF16), peer_v[l].astype(_BF16))

    q_n, kc, vc, ks, vs, kw, vw, g_n, q1, q2, kd, vd, g_m = mixer_project(x_prompt, pos_p, w_in[l])
    o_n, o_d = prompt_attention(q_n, kc, vc, ks, vs, kw, vw, g_n, q1, q2, kd, vd, cmp_k, cmp_v,
                                lam, diff_subln_g[l], lam_init)
    mix = mixer_output(o_n, o_d, g_m, w_br_nsa[l], w_br_diff[l], w_out[l])
    keep = min(WINDOW, seq)
    outs_p = (kd, vd, kc, vc, ks, vs, kw[:, seq - keep:], vw[:, seq - keep:])
    xp = post_block(x_prompt, mix, ln1_g[l], ln1_b[l], ln2_g[l], ln2_b[l], peer_w)

    q_n, kc, vc, ks, vs, kw, vw, g_n, q1, q2, kd, vd, g_m = mixer_project(x_sample, pos_s, w_in[l])
    o_n, win_k_new, win_v_new = nsa_sample(q_n, kc, vc, ks, vs, kw, vw, g_n,
                                           cache_nsa_cmp_k, cache_nsa_cmp_v, cache_nsa_slc_k, cache_nsa_slc_v,
                                           state_nsa_win_k[l], state_nsa_win_v[l], page_table, l, cmp_k, cmp_v)
    nb, n_pool = x_sample.shape[0], cache_diff_k.shape[1]
    o_d = diff_sample_pallas(q1[:, 0], q2[:, 0], kd.reshape(nb, -1), vd.reshape(nb, -1),
                             cache_diff_k[l].reshape(n_pool, PAGE_SIZE, -1),
                             cache_diff_v[l].reshape(n_pool, PAGE_SIZE, -1),
                             page_table, lam, diff_subln_g[l], lam_init)[:, None]
    mix = mixer_output(o_n, o_d, g_m, w_br_nsa[l], w_br_diff[l], w_out[l])
    outs_s = (kd, vd, kc, vc, ks, vs, win_k_new, win_v_new)
    xs = post_block(x_sample, mix, ln1_g[l], ln1_b[l], ln2_g[l], ln2_b[l], peer_w)

    return (xp, xs) + tuple(a[None] for a in outs_p) + tuple(a[None] for a in outs_s)
```

```python
import functools
import math

import jax
import jax.numpy as jnp
from jax import lax
import numpy as np
from jax.experimental import pallas as pl
from jax.experimental.pallas import tpu as pltpu

D_MODEL = 2048
DEPTH = 1
PAGE_SIZE = 128
H_NSA = 16
G_NSA = 2
HPG_NSA = H_NSA // G_NSA
DH_NSA = 64
L_CMP = 32
STRIDE_CMP = 16
CMP_HID = 2 * DH_NSA
L_SLC = 64
N_SEL = 16
WINDOW = 512
H_DIFF = 8
DH_DIFF = 64
DV_DIFF = 2 * DH_DIFF
N_KEYS = 128
N_EXPERTS = N_KEYS * N_KEYS
PEER_HEADS = 8
PEER_DQ = 256
PEER_TOPK = 16
ROPE_THETA = 10000.0
Q_BLOCK = 128
LN_EPS = 1e-5
NEG_INF = -1e30
SEL_BONUS = 1e6
ALPHA = (2.0 * DEPTH) ** 0.25

W_NSA_Q = H_NSA * DH_NSA
W_NSA_KV = G_NSA * DH_NSA
W_NSA_GATE = H_NSA * 3
W_DIFF_QK = H_DIFF * 2 * DH_DIFF
W_DIFF_V = H_DIFF * DV_DIFF
COL_SIZES = (W_NSA_Q, W_NSA_KV, W_NSA_KV, W_NSA_KV, W_NSA_KV, W_NSA_KV, W_NSA_KV, W_NSA_GATE,
             W_DIFF_QK, W_DIFF_QK, W_DIFF_V, 2 * D_MODEL)


def layer_norm(x, g, b):
    xf = x.astype(jnp.float32)
    mu = xf.mean(-1, keepdims=True)
    var = jnp.square(xf - mu).mean(-1, keepdims=True)
    y = (xf - mu) * lax.rsqrt(var + LN_EPS) * g.astype(jnp.float32) + b.astype(jnp.float32)
    return y.astype(x.dtype)


def rope(x, pos):
    d = x.shape[-1]
    half = d // 2
    inv = ROPE_THETA ** (-jnp.arange(half, dtype=jnp.float32) / half)
    ang = pos.astype(jnp.float32)[:, None] * inv[None, :]
    cos, sin = jnp.cos(ang)[None, :, None, :], jnp.sin(ang)[None, :, None, :]
    xf = x.astype(jnp.float32)
    x1, x2 = xf[..., :half], xf[..., half:]
    return jnp.concatenate([x1 * cos - x2 * sin, x1 * sin + x2 * cos], -1).astype(x.dtype)


def masked_softmax(s, mask):
    s = jnp.where(mask, s.astype(jnp.float32), NEG_INF)
    return jnp.where(mask, jax.nn.softmax(s, axis=-1), 0.0)


def split_cols(z):
    out, start = [], 0
    for size in COL_SIZES:
        out.append(z[..., start:start + size])
        start += size
    return out


def gather_pages(cache, layer, page_table):
    b, n_pages = page_table.shape
    rows = cache[layer, page_table]
    return rows.reshape((b, n_pages * PAGE_SIZE) + cache.shape[3:])


def mixer_project(x, pos, w_in):
    b, t, _ = x.shape
    q_n, kc, vc, ks, vs, kw, vw, g_n, q_d, k_d, v_d, g_m = split_cols(x @ w_in)
    kvh = lambda a: a.reshape(b, t, G_NSA, DH_NSA)
    q_n = rope(q_n.reshape(b, t, H_NSA, DH_NSA), pos).reshape(b, t, G_NSA, HPG_NSA, DH_NSA)
    kc, ks, kw = rope(kvh(kc), pos), rope(kvh(ks), pos), rope(kvh(kw), pos)
    vc, vs, vw = kvh(vc), kvh(vs), kvh(vw)
    g_n = jax.nn.sigmoid(g_n.reshape(b, t, H_NSA, 3))
    q_d = q_d.reshape(b, t, H_DIFF, 2, DH_DIFF)
    k_d = k_d.reshape(b, t, H_DIFF, 2, DH_DIFF)
    q1, q2 = rope(q_d[:, :, :, 0], pos), rope(q_d[:, :, :, 1], pos)
    k_d = jnp.concatenate([rope(k_d[:, :, :, 0], pos), rope(k_d[:, :, :, 1], pos)], -1)
    v_d = v_d.reshape(b, t, H_DIFF, DV_DIFF)
    g_m = jax.nn.sigmoid(g_m.reshape(b, t, 2, D_MODEL))
    return (q_n, kc, vc, ks, vs, kw, vw, g_n, q1, q2, k_d, v_d, g_m)


def compress(rows, w1, pe, w2):
    b, t = rows.shape[:2]
    nc = (t - L_CMP) // STRIDE_CMP + 1
    r = rows[:, :STRIDE_CMP * (nc + 1)].reshape(b, nc + 1, STRIDE_CMP, G_NSA, DH_NSA)
    first = jnp.einsum('bnlgd,lde->bnge', r, w1[:STRIDE_CMP])
    second = jnp.einsum('bnlgd,lde->bnge', r, w1[STRIDE_CMP:])
    pe_term = jnp.einsum('ld,lde->e', pe, w1)
    hid = jax.nn.gelu(first[:, :-1] + second[:, 1:] + pe_term, approximate=False)
    return jnp.einsum('bnge,ed->bngd', hid, w2)


def nsa_attend(q, qpos, kc, vc, sel_fn, kw, vw, wpos, gates, n_slc):
    b, tq = q.shape[:2]
    nc = kc.shape[1]
    scale = DH_NSA ** -0.5
    t = qpos[:, None]
    cend = jnp.arange(nc, dtype=jnp.int32) * STRIDE_CMP + (L_CMP - 1)
    s = jnp.einsum('btghd,bngd->btghn', q, kc).astype(jnp.float32) * scale
    p_cmp = masked_softmax(s, (cend[None, :] <= t)[None, :, None, None, :])
    o_cmp = jnp.einsum('btghn,bngd->btghd', p_cmp.astype(vc.dtype), vc)
    ratio = L_SLC // STRIDE_CMP
    imp = jnp.pad(p_cmp.sum(axis=3), ((0, 0), (0, 0), (0, 0), (0, n_slc * ratio - nc)))
    imp = imp.reshape(b, tq, G_NSA, n_slc, ratio).sum(-1)
    blk = jnp.arange(n_slc, dtype=jnp.int32)[None, :]
    cur = (qpos // L_SLC)[:, None]
    forced = (blk == 0) | (blk == cur) | (blk == cur - 1)
    future = blk * L_SLC > t
    score = jnp.where(future[None, :, None, :], -SEL_BONUS,
                      imp + jnp.where(forced, SEL_BONUS, 0.0)[None, :, None, :])
    vals, idx = lax.top_k(score, min(N_SEL, n_slc))
    valid = vals > -0.5 * SEL_BONUS
    spos = idx[..., None] * L_SLC + jnp.arange(L_SLC, dtype=jnp.int32)
    ks, vs = sel_fn(spos)
    smask = (valid[..., None] & (spos <= qpos[None, :, None, None, None])).reshape(b, tq, G_NSA, -1)
    ks = ks.reshape(b, tq, G_NSA, -1, DH_NSA)
    vs = vs.reshape(b, tq, G_NSA, -1, DH_NSA)
    s = jnp.einsum('btghd,btgkd->btghk', q, ks).astype(jnp.float32) * scale
    p = masked_softmax(s, smask[:, :, :, None, :])
    o_slc = jnp.einsum('btghk,btgkd->btghd', p.astype(vs.dtype), vs)
    dist = t - wpos[None, :]
    wmask = (dist >= 0) & (dist <= WINDOW) & (wpos[None, :] >= 0)
    s = jnp.einsum('btghd,bsgd->btghs', q, kw).astype(jnp.float32) * scale
    p = masked_softmax(s, wmask[None, :, None, None, :])
    o_win = jnp.einsum('btghs,bsgd->btghd', p.astype(vw.dtype), vw)
    g = gates.reshape(b, tq, G_NSA, HPG_NSA, 3)
    o = g[..., 0:1] * o_cmp + g[..., 1:2] * o_slc + g[..., 2:3] * o_win
    return o.reshape(b, tq, H_NSA * DH_NSA)


def nsa_prompt(q, kc_rows, vc_rows, ks_rows, vs_rows, kw_rows, vw_rows, gates, cmp_k, cmp_v):
    b, t = q.shape[:2]
    kc, vc = compress(kc_rows, *cmp_k), compress(vc_rows, *cmp_v)
    n_slc = -(-t // L_SLC)
    bidx = jnp.arange(b)[:, None, None, None, None]
    gidx = jnp.arange(G_NSA)[None, None, :, None, None]

    def sel_fn(spos):
        p = jnp.clip(spos, 0, t - 1)
        return ks_rows[bidx, p, gidx], vs_rows[bidx, p, gidx]

    pad = ((0, 0), (WINDOW, 0), (0, 0), (0, 0))
    kw_pad, vw_pad = jnp.pad(kw_rows, pad), jnp.pad(vw_rows, pad)
    n_blk = t // Q_BLOCK
    q_blk = q.reshape(b, n_blk, Q_BLOCK, G_NSA, HPG_NSA, DH_NSA).swapaxes(0, 1)
    g_blk = gates.reshape(b, n_blk, Q_BLOCK, H_NSA, 3).swapaxes(0, 1)

    def one_block(args):
        i, qb, gb = args
        q0 = i * Q_BLOCK
        qpos = q0 + jnp.arange(Q_BLOCK, dtype=jnp.int32)
        wpos = q0 - WINDOW + jnp.arange(WINDOW + Q_BLOCK, dtype=jnp.int32)
        kw = lax.dynamic_slice_in_dim(kw_pad, q0, WINDOW + Q_BLOCK, axis=1)
        vw = lax.dynamic_slice_in_dim(vw_pad, q0, WINDOW + Q_BLOCK, axis=1)
        return nsa_attend(qb, qpos, kc, vc, sel_fn, kw, vw, wpos, gb, n_slc)

    out = lax.map(one_block, (jnp.arange(n_blk, dtype=jnp.int32), q_blk, g_blk))
    return out.swapaxes(0, 1).reshape(b, t, H_NSA * DH_NSA)


def nsa_sample(q, kc_new, vc_new, ks_new, vs_new, kw_new, vw_new, gates, cache_cmp_k, cache_cmp_v,
               cache_slc_k, cache_slc_v, win_k, win_v, page_table, layer, cmp_k, cmp_v):
    b, tn = q.shape[:2]
    past_len = page_table.shape[1] * PAGE_SIZE
    t_total = past_len + tn
    kc = compress(jnp.concatenate([gather_pages(cache_cmp_k, layer, page_table), kc_new], 1), *cmp_k)
    vc = compress(jnp.concatenate([gather_pages(cache_cmp_v, layer, page_table), vc_new], 1), *cmp_v)
    bidx = jnp.arange(b)[:, None, None, None, None]
    gidx = jnp.arange(G_NSA)[None, None, :, None, None]

    def sel_fn(spos):
        in_past = (spos < past_len)[..., None]
        pp = jnp.clip(spos, 0, past_len - 1)
        page = page_table[bidx, pp // PAGE_SIZE]
        off = pp % PAGE_SIZE
        pn = jnp.clip(spos - past_len, 0, tn - 1)
        k = jnp.where(in_past, cache_slc_k[layer, page, off, gidx], ks_new[bidx, pn, gidx])
        v = jnp.where(in_past, cache_slc_v[layer, page, off, gidx], vs_new[bidx, pn, gidx])
        return k, v

    wbuf = win_k.shape[1]
    kw = jnp.concatenate([win_k, kw_new], 1)
    vw = jnp.concatenate([win_v, vw_new], 1)
    wpos = past_len - wbuf + jnp.arange(wbuf + tn, dtype=jnp.int32)
    qpos = past_len + jnp.arange(tn, dtype=jnp.int32)
    o = nsa_attend(q, qpos, kc, vc, sel_fn, kw, vw, wpos, gates, -(-t_total // L_SLC))
    keep = min(WINDOW, t_total)
    return o, kw[:, wbuf + tn - keep:], vw[:, wbuf + tn - keep:]


def diff_attend(q1, q2, k, v, qpos, kpos, lam, subln_g, lam_init):
    b, tq = q1.shape[:2]
    scale = DH_DIFF ** -0.5
    mask = (kpos[None, :] <= qpos[:, None])[None, None]
    k1, k2 = k[..., :DH_DIFF], k[..., DH_DIFF:]
    a1 = masked_softmax(jnp.einsum('bthd,bshd->bhts', q1, k1).astype(jnp.float32) * scale, mask)
    a2 = masked_softmax(jnp.einsum('bthd,bshd->bhts', q2, k2).astype(jnp.float32) * scale, mask)
    o = jnp.einsum('bhts,bshd->bthd', (a1 - lam * a2).astype(v.dtype), v).astype(jnp.float32)
    o = o * lax.rsqrt(jnp.mean(jnp.square(o), -1, keepdims=True) + LN_EPS) * subln_g.astype(jnp.float32)
    return (o * (1.0 - lam_init)).astype(v.dtype).reshape(b, tq, H_DIFF * DV_DIFF)


def diff_prompt(q1, q2, k, v, lam, subln_g, lam_init):
    b, t = q1.shape[:2]
    n_blk = t // Q_BLOCK
    kpos = jnp.arange(t, dtype=jnp.int32)
    qb1 = q1.reshape(b, n_blk, Q_BLOCK, H_DIFF, DH_DIFF).swapaxes(0, 1)
    qb2 = q2.reshape(b, n_blk, Q_BLOCK, H_DIFF, DH_DIFF).swapaxes(0, 1)

    def one_block(args):
        i, a, c = args
        qpos = i * Q_BLOCK + jnp.arange(Q_BLOCK, dtype=jnp.int32)
        return diff_attend(a, c, k, v, qpos, kpos, lam, subln_g, lam_init)

    out = lax.map(one_block, (jnp.arange(n_blk, dtype=jnp.int32), qb1, qb2))
    return out.swapaxes(0, 1).reshape(b, t, H_DIFF * DV_DIFF)


def mixer_output(o_nsa, o_diff, g_m, w_br_nsa, w_br_diff, w_out):
    m = g_m[:, :, 0] * (o_nsa @ w_br_nsa) + g_m[:, :, 1] * (o_diff @ w_br_diff)
    return m @ w_out


def peer_ffn(xf, wq, subkeys, u, v):
    n = xf.shape[0]
    q = (xf @ wq).reshape(n, PEER_HEADS, 2, PEER_DQ // 2)
    s = jnp.einsum('nhpc,hpkc->nhpk', q, subkeys).astype(jnp.float32)
    s1, i1 = lax.top_k(s[:, :, 0], PEER_TOPK)
    s2, i2 = lax.top_k(s[:, :, 1], PEER_TOPK)
    cand = (s1[..., :, None] + s2[..., None, :]).reshape(n, PEER_HEADS, PEER_TOPK * PEER_TOPK)
    cid = (i1[..., :, None] * N_KEYS + i2[..., None, :]).reshape(n, PEER_HEADS, PEER_TOPK * PEER_TOPK)
    top, sel = lax.top_k(cand, PEER_TOPK)
    eid = jnp.take_along_axis(cid, sel, axis=-1)
    g = jax.nn.softmax(top, axis=-1)
    act = jax.nn.gelu(jnp.einsum('nd,nhkd->nhk', xf, u[eid]), approximate=False)
    return jnp.einsum('nhk,nhkd->nd', (g * act.astype(jnp.float32)).astype(xf.dtype), v[eid])


_BF16 = jnp.bfloat16
_F32 = jnp.float32
_VMEM_LIMIT = 56 * 1024 * 1024


def _dot_nt(a, b):
    return lax.dot_general(a, b, (((1,), (1,)), ((), ())), preferred_element_type=_F32)


def _dot(a, b):
    return jnp.dot(a, b, preferred_element_type=_F32)


def _dot_exact01(a, m01):
    hi = a.astype(_BF16)
    r1 = a - hi.astype(_F32)
    mid = r1.astype(_BF16)
    lo = (r1 - mid.astype(_F32)).astype(_BF16)
    return _dot(hi, m01) + _dot(mid, m01) + _dot(lo, m01)


def _gelu_erf(x):
    return 0.5 * x * (1.0 + lax.erf(x * (2.0 ** -0.5)))


def _softmax_rows(s, mask):
    s = jnp.where(mask, s, NEG_INF)
    m = jnp.max(s, axis=-1, keepdims=True)
    e = jnp.where(mask, jnp.exp(s - m), 0.0)
    l = jnp.sum(e, axis=-1, keepdims=True)
    return e / jnp.where(l > 0.0, l, 1.0)


_PJ_TN = 512
_PJ_TM = 1024
_OFF_QN, _OFF_QD, _OFF_KD, _OFF_VD = 0, W_NSA_Q, W_NSA_Q + W_DIFF_QK, W_NSA_Q + 2 * W_DIFF_QK
_OFF_GM = _OFF_VD + W_DIFF_V
_OFF_KV = _OFF_GM + 2 * D_MODEL
_OFF_GN = _OFF_KV + 6 * W_NSA_KV
_GN_PAD = 256
_PJ_COLS = _OFF_GN + _GN_PAD


def regroup_w_in(w):
    c = np.cumsum((0,) + COL_SIZES)
    seg = lambda i: w[:, c[i]:c[i + 1]]
    pad = jnp.zeros((w.shape[0], _GN_PAD - W_NSA_GATE), w.dtype)
    order = [seg(0), seg(8), seg(9), seg(10), seg(11)] + [seg(i) for i in range(1, 7)] + [seg(7), pad]
    return jnp.concatenate(order, axis=1).astype(_BF16)


def rope_tables(pos):
    half = DH_NSA // 2
    inv = ROPE_THETA ** (-jnp.arange(half, dtype=_F32) / half)
    ang = pos.astype(_F32)[:, None] * inv[None, :]
    cos, sin = jnp.cos(ang), jnp.sin(ang)
    return jnp.concatenate([cos, cos, cos, cos], 1), jnp.concatenate([-sin, sin, -sin, sin], 1)


def _rope_chunk(y, cos, sin_signed):
    lane = lax.broadcasted_iota(jnp.int32, y.shape, 1)
    partner = jnp.where(lane % DH_NSA < DH_NSA // 2, pltpu.roll(y, 128 - DH_NSA // 2, 1),
                        pltpu.roll(y, DH_NSA // 2, 1))
    return y * cos + partner * sin_signed


def _proj_kernel(x_ref, w_ref, cos_ref, sin_ref, *o_refs, kind):
    y = _dot(x_ref[...], w_ref[...])
    n_chunk = y.shape[1] // 128
    chunks = [y[:, c * 128:(c + 1) * 128] for c in range(n_chunk)]
    if kind in ("rope_bf16", "rope_f32_bf16"):
        cos, sin = cos_ref[...], sin_ref[...]
        chunks = [_rope_chunk(ch, cos, sin) for ch in chunks]
    if kind == "sigmoid_f32":
        o_refs[0][...] = jax.nn.sigmoid(y)
    elif kind == "rope_bf16":
        o_refs[0][...] = jnp.concatenate(chunks, axis=1).astype(_BF16)
    elif kind in ("rope_f32_bf16", "raw_f32_bf16"):
        z = jnp.concatenate(chunks, axis=1)
        o_refs[0][...] = z
        o_refs[1][...] = z.astype(_BF16)
    elif kind == "kv":
        kf_ref, vf_ref, kb_ref, vb_ref = o_refs
        cos, sin = cos_ref[...], sin_ref[...]
        for p in range(n_chunk // 2):
            kx = _rope_chunk(chunks[2 * p], cos, sin)
            vx = chunks[2 * p + 1]
            kf_ref[p] = kx
            vf_ref[p] = vx
            for g in range(G_NSA):
                kb_ref[p, g] = kx[:, g * DH_NSA:(g + 1) * DH_NSA].astype(_BF16)
                vb_ref[p, g] = vx[:, g * DH_NSA:(g + 1) * DH_NSA].astype(_BF16)


def _proj_call(xb, wb, cos, sin, kind, col_off, n_cols, tn, out_shapes, out_specs, name):
    t, d = xb.shape
    tm = min(_PJ_TM, t)
    off = col_off // tn
    return pl.pallas_call(
        functools.partial(_proj_kernel, kind=kind),
        grid=(n_cols // tn, t // tm),
        in_specs=[pl.BlockSpec((tm, d), lambda j, i: (i, 0)),
                  pl.BlockSpec((d, tn), lambda j, i: (0, off + j)),
                  pl.BlockSpec((tm, 128), lambda j, i: (i, 0)),
                  pl.BlockSpec((tm, 128), lambda j, i: (i, 0))],
        out_specs=out_specs(tm, tn),
        out_shape=out_shapes,
        compiler_params=pltpu.CompilerParams(
            dimension_semantics=("arbitrary", "arbitrary"), vmem_limit_bytes=_VMEM_LIMIT),
        name=name,
    )(xb, wb, cos, sin)


def mixer_project_pallas(x, pos, wb):
    t = x.shape[0]
    xb = x.astype(_BF16)
    cos, sin = rope_tables(pos)
    sds = jax.ShapeDtypeStruct
    plain = lambda tm, tn: pl.BlockSpec((tm, tn), lambda j, i: (i, j))
    one = lambda dt: lambda tm, tn: [plain(tm, tn)]
    two = lambda tm, tn: [plain(tm, tn), plain(tm, tn)]
    tn = _PJ_TN
    q_n, = _proj_call(xb, wb, cos, sin, "rope_bf16", _OFF_QN, W_NSA_Q, tn,
                      [sds((t, W_NSA_Q), _BF16)], one(_BF16), "proj_nsa_q")
    q_d, = _proj_call(xb, wb, cos, sin, "rope_bf16", _OFF_QD, W_DIFF_QK, tn,
                      [sds((t, W_DIFF_QK), _BF16)], one(_BF16), "proj_diff_q")
    k_d, k_db = _proj_call(xb, wb, cos, sin, "rope_f32_bf16", _OFF_KD, W_DIFF_QK, tn,
                           [sds((t, W_DIFF_QK), _F32), sds((t, W_DIFF_QK), _BF16)], two, "proj_diff_k")
    v_d, v_db = _proj_call(xb, wb, cos, sin, "raw_f32_bf16", _OFF_VD, W_DIFF_V, tn,
                           [sds((t, W_DIFF_V), _F32), sds((t, W_DIFF_V), _BF16)], two, "proj_diff_v")
    g_m, = _proj_call(xb, wb, cos, sin, "sigmoid_f32", _OFF_GM, 2 * D_MODEL, tn,
                      [sds((t, 2 * D_MODEL), _F32)], one(_F32), "proj_merge_gates")
    g_n, = _proj_call(xb, wb, cos, sin, "sigmoid_f32", _OFF_GN, _GN_PAD, _GN_PAD,
                      [sds((t, _GN_PAD), _F32)], one(_F32), "proj_nsa_gates")
    kv_tn = 2 * W_NSA_KV
    kv_specs = lambda tm, tn_: [
        pl.BlockSpec((1, tm, W_NSA_KV), lambda j, i: (j, i, 0)),
        pl.BlockSpec((1, tm, W_NSA_KV), lambda j, i: (j, i, 0)),
        pl.BlockSpec((1, G_NSA, tm, DH_NSA), lambda j, i: (j, 0, i, 0)),
        pl.BlockSpec((1, G_NSA, tm, DH_NSA), lambda j, i: (j, 0, i, 0))]
    kf, vf, kb, vb = _proj_call(
        xb, wb, cos, sin, "kv", _OFF_KV, 6 * W_NSA_KV, kv_tn,
        [sds((3, t, W_NSA_KV), _F32), sds((3, t, W_NSA_KV), _F32),
         sds((3, G_NSA, t, DH_NSA), _BF16), sds((3, G_NSA, t, DH_NSA), _BF16)], kv_specs, "proj_nsa_kv")
    return dict(q_n=q_n, q_d=q_d, k_d=k_d, k_db=k_db, v_d=v_d, v_db=v_db, g_m=g_m,
                g_n=g_n[:, :W_NSA_GATE], kf=kf, vf=vf, kb=kb, vb=vb)


def _merge_kernel(on_ref, od_ref, g0_ref, g1_ref, wn_ref, wd_ref, m_ref):
    m = g0_ref[...] * _dot(on_ref[...], wn_ref[...]) + g1_ref[...] * _dot(od_ref[...], wd_ref[...])
    m_ref[...] = m.astype(_BF16)


def _out_ln_kernel(m_ref, wo_ref, x_ref, g_ref, b_ref, h_ref):
    y = ALPHA * x_ref[...] + _dot(m_ref[...], wo_ref[...])
    mu = jnp.mean(y, axis=-1, keepdims=True)
    yc = y - mu
    var = jnp.mean(yc * yc, axis=-1, keepdims=True)
    h_ref[...] = yc * lax.rsqrt(var + LN_EPS) * g_ref[...] + b_ref[...]


def mixer_output_ln_pallas(o_n, o_d, g_m, x, wn, wd, wo, ln_g, ln_b):
    t, d = x.shape
    tm = min(_PJ_TM, t)
    tn = _PJ_TN
    nj = d // tn
    m = pl.pallas_call(
        _merge_kernel,
        grid=(nj, t // tm),
        in_specs=[pl.BlockSpec((tm, o_n.shape[1]), lambda j, i: (i, 0)),
                  pl.BlockSpec((tm, o_d.shape[1]), lambda j, i: (i, 0)),
                  pl.BlockSpec((tm, tn), lambda j, i: (i, j)),
                  pl.BlockSpec((tm, tn), lambda j, i: (i, nj + j)),
                  pl.BlockSpec((wn.shape[0], tn), lambda j, i: (0, j)),
                  pl.BlockSpec((wd.shape[0], tn), lambda j, i: (0, j))],
        out_specs=pl.BlockSpec((tm, tn), lambda j, i: (i, j)),
        out_shape=jax.ShapeDtypeStruct((t, d), _BF16),
        compiler_params=pltpu.CompilerParams(
            dimension_semantics=("arbitrary", "arbitrary"), vmem_limit_bytes=_VMEM_LIMIT),
        name="mixer_merge",
    )(o_n, o_d, g_m, g_m, wn, wd)
    tm2 = min(512, t)
    return pl.pallas_call(
        _out_ln_kernel,
        grid=(t // tm2,),
        in_specs=[pl.BlockSpec((tm2, d), lambda i: (i, 0)),
                  pl.BlockSpec((d, d), lambda i: (0, 0)),
                  pl.BlockSpec((tm2, d), lambda i: (i, 0)),
                  pl.BlockSpec((1, d), lambda i: (0, 0)),
                  pl.BlockSpec((1, d), lambda i: (0, 0))],
        out_specs=pl.BlockSpec((tm2, d), lambda i: (i, 0)),
        out_shape=jax.ShapeDtypeStruct((t, d), _F32),
        compiler_params=pltpu.CompilerParams(
            dimension_semantics=("arbitrary",), vmem_limit_bytes=_VMEM_LIMIT),
        name="mixer_out_ln",
    )(m, wo, x, ln_g.reshape(1, d), ln_b.reshape(1, d))


def _compress_kernel(r_ref, w1a_ref, w1b_ref, pe_ref, w1_ref, w2_ref, o_ref):
    r = r_ref[...]
    first = _dot(r, w1a_ref[...])
    second = _dot(r, w1b_ref[...])
    nb = first.shape[0]
    second_next = jnp.concatenate([second[1:], jnp.zeros((1, CMP_HID), _F32)], axis=0)
    pe_term = _dot(pe_ref[...], w1_ref[...])[0:1]
    hid = _gelu_erf(first + second_next + pe_term)
    o_ref[...] = _dot(hid.astype(_BF16), w2_ref[...])


def compress_pallas(rows, w1, pe, w2):
    t = rows.shape[0]
    nb = t // STRIDE_CMP
    r = rows.reshape(nb, STRIDE_CMP, G_NSA, DH_NSA).transpose(2, 0, 1, 3)
    r = r.reshape(G_NSA, nb, STRIDE_CMP * DH_NSA).astype(_BF16)
    w1f = w1.reshape(L_CMP * DH_NSA, CMP_HID).astype(_BF16)
    half = STRIDE_CMP * DH_NSA
    pe8 = jnp.broadcast_to(pe.reshape(1, L_CMP * DH_NSA), (8, L_CMP * DH_NSA)).astype(_BF16)
    full = lambda shape: pl.BlockSpec(shape, lambda g: (0,) * len(shape))
    return pl.pallas_call(
        _compress_kernel,
        grid=(G_NSA,),
        in_specs=[pl.BlockSpec((None, nb, half), lambda g: (g, 0, 0)),
                  full((half, CMP_HID)), full((half, CMP_HID)),
                  full((8, 2 * half)), full((2 * half, CMP_HID)), full((CMP_HID, DH_NSA))],
        out_specs=pl.BlockSpec((None, nb, DH_NSA), lambda g: (g, 0, 0)),
        out_shape=jax.ShapeDtypeStruct((G_NSA, nb, DH_NSA), _F32),
        name="nsa_compress",
    )(r, w1f[:half], w1f[half:], pe8, w1f, w2.astype(_BF16))


_NSA_TQ = 128
_NSA_TK = 512


def _nsa_prompt_kernel(q_ref, kc_ref, vc_ref, ks_ref, vs_ref, kw_ref, vw_ref, g_ref, o_ref, st_ref,
                       *, nc, t_total):
    tq, tk, hh = _NSA_TQ, _NSA_TK, HPG_NSA
    q0 = pl.program_id(1) * tq
    qt = q_ref[...]
    q = jnp.concatenate([qt[:, h * DH_NSA:(h + 1) * DH_NSA] for h in range(hh)], axis=0)
    q = (q * (DH_NSA ** -0.5)).astype(_BF16)
    tpos = q0 + lax.broadcasted_iota(jnp.int32, (tq, 1), 0)

    ncp = kc_ref.shape[0]
    s = _dot_nt(q, kc_ref[...]).reshape(hh, tq, ncp)
    n_idx = lax.broadcasted_iota(jnp.int32, (tq, ncp), 1)
    cmask = ((n_idx * STRIDE_CMP + (L_CMP - 1)) <= tpos) & (n_idx < nc)
    p = _softmax_rows(s, cmask[None])
    o_cmp = _dot(p.reshape(hh * tq, ncp).astype(_BF16), vc_ref[...])
    ratio = L_SLC // STRIDE_CMP
    n_slc = ncp // ratio
    pool = (lax.broadcasted_iota(jnp.int32, (ncp, n_slc), 0) // ratio
            == lax.broadcasted_iota(jnp.int32, (ncp, n_slc), 1)).astype(_BF16)
    imp = _dot_exact01(jnp.sum(p, axis=0), pool)

    blk = lax.broadcasted_iota(jnp.int32, (tq, n_slc), 1)
    cur = tpos // L_SLC
    forced = (blk == 0) | (blk == cur) | (blk == cur - 1)
    future = blk * L_SLC > tpos
    score = jnp.where(future, -SEL_BONUS, imp + jnp.where(forced, SEL_BONUS, 0.0))
    st_ref[...] = score.T
    st = st_ref[...]
    b_idx = lax.broadcasted_iota(jnp.int32, (n_slc, tq), 0)

    def rank_body(bp, cnt):
        row = st_ref[pl.ds(bp, 1), :]
        ahead = (row > st) | ((row == st) & (b_idx > bp))
        return cnt + jnp.where(ahead, 1.0, 0.0)

    cnt = lax.fori_loop(0, n_slc, rank_body, jnp.zeros((n_slc, tq), _F32), unroll=8)
    sel_t = (cnt < float(N_SEL)) & (st > -0.5 * SEL_BONUS)
    sel = sel_t.astype(_F32).T.astype(_BF16)

    def slc_body(c, carry):
        m, l, acc = carry
        k0 = pl.multiple_of(c * tk, tk)
        kk = ks_ref[pl.ds(k0, tk), :]
        vv = vs_ref[pl.ds(k0, tk), :]
        sc = _dot_nt(q, kk).reshape(hh, tq, tk)
        kpos = k0 + lax.broadcasted_iota(jnp.int32, (1, tk), 1)
        expand = ((k0 + lax.broadcasted_iota(jnp.int32, (n_slc, tk), 1)) // L_SLC
                  == lax.broadcasted_iota(jnp.int32, (n_slc, tk), 0)).astype(_BF16)
        mk = (_dot(sel, expand) > 0.5) & (kpos <= tpos)
        mk = mk[None]
        sc = jnp.where(mk, sc, NEG_INF)
        m_new = jnp.maximum(m, jnp.max(sc, axis=-1, keepdims=True))
        alpha = jnp.exp(m - m_new)
        pp = jnp.where(mk, jnp.exp(sc - m_new), 0.0)
        l = alpha * l + jnp.sum(pp, axis=-1, keepdims=True)
        acc = acc * alpha.reshape(hh * tq, 1) + _dot(pp.reshape(hh * tq, tk).astype(_BF16), vv)
        return m_new, l, acc

    n_chunks = (q0 + tq + tk - 1) // tk
    m0 = jnp.full((hh, tq, 1), NEG_INF, _F32)
    l0 = jnp.zeros((hh, tq, 1), _F32)
    a0 = jnp.zeros((hh * tq, DH_NSA), _F32)
    _, l_s, acc_s = lax.fori_loop(0, n_chunks, slc_body, (m0, l0, a0))
    o_slc = acc_s / jnp.where(l_s > 0.0, l_s, 1.0).reshape(hh * tq, 1)

    wlen = WINDOW + tq
    w0 = pl.multiple_of(jnp.maximum(q0 - WINDOW, 0), tq)
    sw = _dot_nt(q, kw_ref[pl.ds(w0, wlen), :]).reshape(hh, tq, wlen)
    dist = tpos - (w0 + lax.broadcasted_iota(jnp.int32, (1, wlen), 1))
    wmask = (dist >= 0) & (dist <= WINDOW)
    pw = _softmax_rows(sw, wmask[None])
    o_win = _dot(pw.reshape(hh * tq, wlen).astype(_BF16), vw_ref[pl.ds(w0, wlen), :])

    g = g_ref[...]
    outs = []
    for h in range(hh):
        rows = slice(h * tq, (h + 1) * tq)
        outs.append(g[:, 3 * h:3 * h + 1] * o_cmp[rows] + g[:, 3 * h + 1:3 * h + 2] * o_slc[rows]
                    + g[:, 3 * h + 2:3 * h + 3] * o_win[rows])
    o_ref[...] = jnp.concatenate(outs, axis=1).astype(o_ref.dtype)


def nsa_prompt_pallas(q, kc, vc, ks_g, vs_g, kw_g, vw_g, gates, nc):
    t = q.shape[0]
    ncp = kc.shape[1]
    tq = _NSA_TQ
    assert t % _NSA_TK == 0 and t >= WINDOW + tq and ncp * STRIDE_CMP == t
    per_group = lambda a: a.astype(_BF16)
    ks_rows, vs_rows, kw_rows, vw_rows = ks_g, vs_g, kw_g, vw_g
    g3 = gates.reshape(t, G_NSA, HPG_NSA * 3).transpose(1, 0, 2)
    hw = HPG_NSA * DH_NSA
    kv_spec = pl.BlockSpec((None, t, DH_NSA), lambda g, i: (g, 0, 0))
    c_spec = pl.BlockSpec((None, ncp, DH_NSA), lambda g, i: (g, 0, 0))
    return pl.pallas_call(
        functools.partial(_nsa_prompt_kernel, nc=nc, t_total=t),
        grid=(G_NSA, t // tq),
        in_specs=[pl.BlockSpec((tq, hw), lambda g, i: (i, g)),
                  c_spec, c_spec, kv_spec, kv_spec, kv_spec, kv_spec,
                  pl.BlockSpec((None, tq, HPG_NSA * 3), lambda g, i: (g, i, 0))],
        out_specs=pl.BlockSpec((tq, hw), lambda g, i: (i, g)),
        out_shape=jax.ShapeDtypeStruct((t, H_NSA * DH_NSA), _BF16),
        scratch_shapes=[pltpu.VMEM((ncp // (L_SLC // STRIDE_CMP), tq), _F32)],
        compiler_params=pltpu.CompilerParams(
            dimension_semantics=("arbitrary", "arbitrary"), vmem_limit_bytes=_VMEM_LIMIT),
        name="nsa_prompt",
    )(q, kc.astype(_BF16), vc.astype(_BF16), per_group(ks_rows), per_group(vs_rows),
      per_group(kw_rows), per_group(vw_rows), g3)


_DIFF_TQ = 256
_DIFF_TK = 512


def _diff_prompt_kernel(lam_ref, q_ref, k_ref, v_ref, gam_ref, o_ref, *, out_scale):
    tq, tk = _DIFF_TQ, _DIFF_TK
    q0 = pl.program_id(1) * tq
    qt = (q_ref[...] * (DH_DIFF ** -0.5)).astype(_BF16)
    q1, q2 = qt[:, :DH_DIFF], qt[:, DH_DIFF:]
    tpos = q0 + lax.broadcasted_iota(jnp.int32, (tq, 1), 0)

    def body(c, carry):
        m1, l1, a1, m2, l2, a2 = carry
        k0 = pl.multiple_of(c * tk, tk)
        kk = k_ref[pl.ds(k0, tk), :]
        vv = v_ref[pl.ds(k0, tk), :]
        mk = (k0 + lax.broadcasted_iota(jnp.int32, (1, tk), 1)) <= tpos

        def stream(qh, kh, m, l, a):
            sc = jnp.where(mk, _dot_nt(qh, kh), NEG_INF)
            m_new = jnp.maximum(m, jnp.max(sc, axis=-1, keepdims=True))
            alpha = jnp.exp(m - m_new)
            pp = jnp.where(mk, jnp.exp(sc - m_new), 0.0)
            l = alpha * l + jnp.sum(pp, axis=-1, keepdims=True)
            a = alpha * a + _dot(pp.astype(_BF16), vv)
            return m_new, l, a

        m1, l1, a1 = stream(q1, kk[:, :DH_DIFF], m1, l1, a1)
        m2, l2, a2 = stream(q2, kk[:, DH_DIFF:], m2, l2, a2)
        return m1, l1, a1, m2, l2, a2

    n_chunks = (q0 + tq + tk - 1) // tk
    mi = jnp.full((tq, 1), NEG_INF, _F32)
    li = jnp.zeros((tq, 1), _F32)
    ai = jnp.zeros((tq, DV_DIFF), _F32)
    _, l1, a1, _, l2, a2 = lax.fori_loop(0, n_chunks, body, (mi, li, ai, mi, li, ai))
    lam = lam_ref[0:1, 0:1]
    o = a1 / l1 - lam * (a2 / l2)
    o = o * lax.rsqrt(jnp.mean(o * o, axis=-1, keepdims=True) + LN_EPS) * gam_ref[...]
    o_ref[...] = (o * out_scale).astype(o_ref.dtype)


def diff_prompt_pallas(q, k, v, lam, subln_g, lam_init):
    t = q.shape[0]
    tq = _DIFF_TQ
    assert t % _DIFF_TK == 0
    lam_row = jnp.full((8, 128), lam, _F32)
    hw = 2 * DH_DIFF
    return pl.pallas_call(
        functools.partial(_diff_prompt_kernel, out_scale=1.0 - lam_init),
        grid=(H_DIFF, t // tq),
        in_specs=[pl.BlockSpec((8, 128), lambda h, i: (0, 0)),
                  pl.BlockSpec((tq, hw), lambda h, i: (i, h)),
                  pl.BlockSpec((t, hw), lambda h, i: (0, h)),
                  pl.BlockSpec((t, DV_DIFF), lambda h, i: (0, h)),
                  pl.BlockSpec((1, DV_DIFF), lambda h, i: (0, 0))],
        out_specs=pl.BlockSpec((tq, DV_DIFF), lambda h, i: (i, h)),
        out_shape=jax.ShapeDtypeStruct((t, H_DIFF * DV_DIFF), _BF16),
        compiler_params=pltpu.CompilerParams(
            dimension_semantics=("arbitrary", "arbitrary"), vmem_limit_bytes=_VMEM_LIMIT),
        name="diff_prompt",
    )(lam_row, q, k.astype(_BF16), v.astype(_BF16), subln_g.reshape(1, DV_DIFF))


_SD_PAGES = 8
_SD_STREAMS = 2 * H_DIFF


def _diff_sample_kernel(pt_ref, lam_ref, q_ref, kn_ref, vn_ref, gam_ref, *refs, out_scale):
    k_refs, v_refs = refs[:_SD_PAGES], refs[_SD_PAGES:2 * _SD_PAGES]
    o_ref, m_ref, l_ref, acc_ref = refs[2 * _SD_PAGES:]
    c = pl.program_id(1)
    q = q_ref[...]
    own_head = lambda n: (lax.broadcasted_iota(jnp.int32, (_SD_STREAMS, n), 1) % H_DIFF
                          == lax.broadcasted_iota(jnp.int32, (_SD_STREAMS, n), 0) // 2)

    @pl.when(c == 0)
    def _():
        s0 = _dot_nt(q, kn_ref[...].astype(_BF16))
        m_ref[...] = jnp.max(jnp.where(own_head(H_DIFF), s0, NEG_INF), axis=-1, keepdims=True)
        l_ref[...] = jnp.ones(l_ref.shape, _F32)
        acc_ref[...] = vn_ref[...].astype(_BF16).astype(_F32)

    rows = PAGE_SIZE * H_DIFF
    flat = lambda r: r[...].reshape(rows, 2 * DH_DIFF).astype(_BF16)
    kk = jnp.concatenate([flat(r) for r in k_refs], axis=0)
    vv = jnp.concatenate([flat(r) for r in v_refs], axis=0)
    mk = own_head(_SD_PAGES * rows)
    sc = jnp.where(mk, _dot_nt(q, kk), NEG_INF)
    m_old = m_ref[...]
    m_new = jnp.maximum(m_old, jnp.max(sc, axis=-1, keepdims=True))
    alpha = jnp.exp(m_old - m_new)
    pp = jnp.where(mk, jnp.exp(sc - m_new), 0.0)
    m_ref[...] = m_new
    l_ref[...] = alpha * l_ref[...] + jnp.sum(pp, axis=-1, keepdims=True)
    acc_ref[...] = alpha * acc_ref[...] + _dot(pp.astype(_BF16), vv)

    @pl.when(c == pl.num_programs(1) - 1)
    def _():
        lam = lam_ref[0:1, 0:1]
        on = acc_ref[...] / l_ref[...]
        outs = []
        for h in range(H_DIFF):
            o = on[2 * h:2 * h + 1, :] - lam * on[2 * h + 1:2 * h + 2, :]
            o = o * lax.rsqrt(jnp.mean(o * o, axis=-1, keepdims=True) + LN_EPS) * gam_ref[...]
            outs.append(o * out_scale)
        o_ref[...] = jnp.concatenate(outs, axis=1)


def diff_sample_pallas(q1, q2, k_new, v_new, cache_k, cache_v, page_table, lam, subln_g, lam_init):
    b, n_pages = page_table.shape
    hw = 2 * DH_DIFF
    assert n_pages % _SD_PAGES == 0 and DV_DIFF == hw
    qs = jnp.stack([q1, q2], axis=2) * (DH_DIFF ** -0.5)
    q16 = jnp.einsum('bhjd,ji->bhjid', qs, jnp.eye(2, dtype=_F32)).reshape(b, _SD_STREAMS, hw).astype(_BF16)
    vn16 = jnp.repeat(v_new, 2, axis=1)
    lam_row = jnp.full((8, 128), lam, _F32)
    page_spec = lambda j: pl.BlockSpec((None, None, PAGE_SIZE, H_DIFF, hw),
                                       lambda bb, c, pt: (0, pt[bb, c * _SD_PAGES + j], 0, 0, 0))
    per_b = lambda rows: pl.BlockSpec((None, rows, hw), lambda bb, c, pt: (bb, 0, 0))
    grid_spec = pltpu.PrefetchScalarGridSpec(
        num_scalar_prefetch=1,
        grid=(b, n_pages // _SD_PAGES),
        in_specs=[pl.BlockSpec((8, 128), lambda bb, c, pt: (0, 0)),
                  per_b(_SD_STREAMS), per_b(H_DIFF), per_b(_SD_STREAMS),
                  pl.BlockSpec((1, DV_DIFF), lambda bb, c, pt: (0, 0))]
                 + [page_spec(j) for j in range(_SD_PAGES)] * 2,
        out_specs=pl.BlockSpec((None, 1, H_DIFF * DV_DIFF), lambda bb, c, pt: (bb, 0, 0)),
        scratch_shapes=[pltpu.VMEM((_SD_STREAMS, 1), _F32), pltpu.VMEM((_SD_STREAMS, 1), _F32),
                        pltpu.VMEM((_SD_STREAMS, DV_DIFF), _F32)],
    )
    out = pl.pallas_call(
        functools.partial(_diff_sample_kernel, out_scale=1.0 - lam_init),
        grid_spec=grid_spec,
        out_shape=jax.ShapeDtypeStruct((b, 1, H_DIFF * DV_DIFF), _F32),
        compiler_params=pltpu.CompilerParams(
            dimension_semantics=("arbitrary", "arbitrary"), vmem_limit_bytes=_VMEM_LIMIT),
        name="diff_sample",
    )(page_table, lam_row, q16, k_new, vn16, subln_g.reshape(1, DV_DIFF),
      *([cache_k] * _SD_PAGES), *([cache_v] * _SD_PAGES))
    return out.reshape(b, H_DIFF * DV_DIFF)


_PEER_PAIRS = tuple((x, y) for x in range(PEER_TOPK) for y in range(PEER_TOPK)
                    if (x + 1) * (y + 1) <= PEER_TOPK)
_PEER_NPAIR_PAD = -(-len(_PEER_PAIRS) // 8) * 8
_PEER_HALF = PEER_DQ // 2


def _distinct_top(x):
    vals, cnts = [], []
    for _ in range(PEER_TOPK):
        m = jnp.max(x, axis=0, keepdims=True)
        eq = x == m
        vals.append(m)
        cnts.append(jnp.sum(jnp.where(eq, 1.0, 0.0), axis=0, keepdims=True))
        x = jnp.where(eq, NEG_INF, x)
    return vals, cnts


def _peer_a_kernel(h_ref, wq_ref, sk_ref, s1_ref, e1_ref, s2_ref, e2_ref, tau_ref, c_ref, w_ref):
    tn = h_ref.shape[0]
    q = _dot(h_ref[...], wq_ref[...]).astype(_BF16)
    cols = lambda hh, p: q[:, (2 * hh + p) * _PEER_HALF:(2 * hh + p + 1) * _PEER_HALF]
    s1 = jnp.concatenate([_dot_nt(sk_ref[2 * hh], cols(hh, 0)) for hh in range(PEER_HEADS)], axis=1)
    s2 = jnp.concatenate([_dot_nt(sk_ref[2 * hh + 1], cols(hh, 1)) for hh in range(PEER_HEADS)], axis=1)
    a, ca = _distinct_top(s1)
    b, cb = _distinct_top(s2)
    c_ref[...] = jnp.full(c_ref.shape, NEG_INF, _F32)
    w_ref[...] = jnp.zeros(w_ref.shape, _F32)
    for k, (x, y) in enumerate(_PEER_PAIRS):
        c_ref[k:k + 1, :] = a[x] + b[y]
        w_ref[k:k + 1, :] = ca[x] * cb[y]
    cand, wgt = c_ref[...], w_ref[...]
    tau = jnp.full(a[0].shape, NEG_INF, _F32)
    for k in range(len(_PEER_PAIRS)):
        cu = c_ref[k:k + 1, :]
        n_ge = jnp.sum(jnp.where(cand >= cu, wgt, 0.0), axis=0, keepdims=True)
        tau = jnp.maximum(tau, jnp.where(n_ge >= float(PEER_TOPK), cu, NEG_INF))
    top = a[0] + b[0]
    z = jnp.sum(jnp.where(cand >= tau, wgt * jnp.exp(cand - top), 0.0), axis=0, keepdims=True)
    e1 = jnp.exp(s1 - a[0])
    e2 = jnp.exp(s2 - b[0]) / z
    for hh in range(PEER_HEADS):
        lanes = slice(hh * tn, (hh + 1) * tn)
        s1_ref[hh] = s1[:, lanes]
        e1_ref[hh] = e1[:, lanes]
        s2_ref[hh] = s2[:, lanes]
        e2_ref[hh] = e2[:, lanes]
        tau_ref[hh:hh + 1, :] = tau[:, lanes]


_PEER_RB = 64
_PEER_PIECE = 256


def _peer_b_kernel(h_ref, s1_ref, e1_ref, s2_ref, e2_ref, tau_ref, u_ref, v_ref, o_ref, g_ref):
    tn = h_ref.shape[0]
    c = pl.program_id(1)
    n_sub = u_ref.shape[0] // N_KEYS
    s1_rows = [[s1_ref[hh, pl.ds(c * n_sub + ii, 1), :] for ii in range(n_sub)] for hh in range(PEER_HEADS)]
    e1_rows = [[e1_ref[hh, pl.ds(c * n_sub + ii, 1), :] for ii in range(n_sub)] for hh in range(PEER_HEADS)]
    w_parts = []
    pw = min(_PEER_PIECE, tn)
    for piece in range(tn // pw):
        tok = slice(piece * pw, (piece + 1) * pw)
        a_t = _dot_nt(u_ref[...], h_ref[tok, :])
        for cb in range(piece * pw // 128, (piece + 1) * pw // 128):
            lanes = slice(cb * 128, (cb + 1) * 128)
            for rb in range(N_KEYS // _PEER_RB):
                rows = slice(rb * _PEER_RB, (rb + 1) * _PEER_RB)
                accs = [jnp.zeros((_PEER_RB, 128), _F32) for _ in range(n_sub)]
                for hh in range(PEER_HEADS):
                    s2 = s2_ref[hh, rows, lanes]
                    e2 = e2_ref[hh, rows, lanes]
                    tau = tau_ref[hh:hh + 1, lanes]
                    for ii in range(n_sub):
                        pair = s2 + s1_rows[hh][ii][:, lanes]
                        accs[ii] = accs[ii] + jnp.where(pair >= tau, e2, 0.0) * e1_rows[hh][ii][:, lanes]
                for ii in range(n_sub):
                    g_ref[ii * N_KEYS + rb * _PEER_RB:ii * N_KEYS + (rb + 1) * _PEER_RB, lanes] = accs[ii]
        w_parts.append((g_ref[:, tok] * _gelu_erf(a_t)).astype(_BF16))
    w_t = jnp.concatenate(w_parts, axis=1)
    contrib = lax.dot_general(w_t, v_ref[...], (((0,), (0,)), ((), ())), preferred_element_type=_F32)

    @pl.when(c == 0)
    def _():
        o_ref[...] = contrib

    @pl.when(c > 0)
    def _():
        o_ref[...] += contrib


_PEER_E_CHUNK = 512
_PEER_TN = 512


def peer_ffn_pallas(h, wq, subkeys, u, v, *, tn):
    n, d = h.shape
    assert n % tn == 0
    hb = h.astype(_BF16)
    r = PEER_HEADS * tn
    stat = jax.ShapeDtypeStruct((PEER_HEADS, N_KEYS, n), _F32)
    stat_spec = pl.BlockSpec((PEER_HEADS, N_KEYS, tn), lambda i: (0, 0, i))
    s1, e1, s2, e2, tau = pl.pallas_call(
        _peer_a_kernel,
        grid=(n // tn,),
        in_specs=[pl.BlockSpec((tn, d), lambda i: (i, 0)),
                  pl.BlockSpec(wq.shape, lambda i: (0, 0)),
                  pl.BlockSpec(subkeys.shape, lambda i: (0, 0, 0))],
        out_specs=[stat_spec, stat_spec, stat_spec, stat_spec,
                   pl.BlockSpec((PEER_HEADS, tn), lambda i: (0, i))],
        out_shape=[stat, stat, stat, stat, jax.ShapeDtypeStruct((PEER_HEADS, n), _F32)],
        scratch_shapes=[pltpu.VMEM((_PEER_NPAIR_PAD, r), _F32), pltpu.VMEM((_PEER_NPAIR_PAD, r), _F32)],
        compiler_params=pltpu.CompilerParams(
            dimension_semantics=("arbitrary",), vmem_limit_bytes=_VMEM_LIMIT),
        name="peer_scores",
    )(hb, wq, subkeys)
    e = _PEER_E_CHUNK
    stat_spec_b = pl.BlockSpec((PEER_HEADS, N_KEYS, tn), lambda i, c: (0, 0, i))
    return pl.pallas_call(
        _peer_b_kernel,
        grid=(n // tn, N_EXPERTS // e),
        in_specs=[pl.BlockSpec((tn, d), lambda i, c: (i, 0)),
                  stat_spec_b, stat_spec_b, stat_spec_b, stat_spec_b,
                  pl.BlockSpec((PEER_HEADS, tn), lambda i, c: (0, i)),
                  pl.BlockSpec((e, d), lambda i, c: (c, 0)),
                  pl.BlockSpec((e, d), lambda i, c: (c, 0))],
        out_specs=pl.BlockSpec((tn, d), lambda i, c: (i, 0)),
        out_shape=jax.ShapeDtypeStruct((n, d), _F32),
        scratch_shapes=[pltpu.VMEM((e, tn), _F32)],
        compiler_params=pltpu.CompilerParams(
            dimension_semantics=("arbitrary", "arbitrary"), vmem_limit_bytes=_VMEM_LIMIT),
        name="peer_experts",
    )(hb, s1, e1, s2, e2, tau, u, v)


def _ln_residual_kernel(a_ref, f_ref, g_ref, b_ref, o_ref):
    x = ALPHA * a_ref[...] + f_ref[...]
    mu = jnp.mean(x, axis=-1, keepdims=True)
    xc = x - mu
    var = jnp.mean(xc * xc, axis=-1, keepdims=True)
    o_ref[...] = xc * lax.rsqrt(var + LN_EPS) * g_ref[...] + b_ref[...]


def ln_residual(a, f, g, b, *, tm=256):
    n, d = a.shape
    tm = min(tm, n)
    return pl.pallas_call(
        _ln_residual_kernel,
        grid=(n // tm,),
        in_specs=[pl.BlockSpec((tm, d), lambda i: (i, 0)),
                  pl.BlockSpec((tm, d), lambda i: (i, 0)),
                  pl.BlockSpec((1, d), lambda i: (0, 0)),
                  pl.BlockSpec((1, d), lambda i: (0, 0))],
        out_specs=pl.BlockSpec((tm, d), lambda i: (i, 0)),
        out_shape=jax.ShapeDtypeStruct((n, d), jnp.float32),
        name="ln_residual",
    )(a, f, g.reshape(1, d), b.reshape(1, d))


def prompt_attention(pj, cmp_k, cmp_v, lam, subln_g, lam_init):
    t = pj["q_n"].shape[0]
    nc = (t - L_CMP) // STRIDE_CMP + 1
    kc_c = compress_pallas(pj["kf"][0], *cmp_k)
    vc_c = compress_pallas(pj["vf"][0], *cmp_v)
    kb, vb = pj["kb"], pj["vb"]
    o_n = nsa_prompt_pallas(pj["q_n"], kc_c, vc_c, kb[1], vb[1], kb[2], vb[2], pj["g_n"], nc)
    o_d = diff_prompt_pallas(pj["q_d"], pj["k_db"], pj["v_db"], lam, subln_g, lam_init)
    return o_n, o_d


def peer_block(h, ln2_g, ln2_b, peer_w):
    n = h.shape[0]
    tn = min(_PEER_TN, -(-n // Q_BLOCK) * Q_BLOCK)
    n_pad = -(-n // tn) * tn
    f = peer_ffn_pallas(jnp.pad(h, ((0, n_pad - n), (0, 0))), *peer_w, tn=tn)[:n]
    return ln_residual(h, f, ln2_g, ln2_b)


def post_block(x, mix, ln1_g, ln1_b, ln2_g, ln2_b, peer_w):
    b, t, d = x.shape
    h = ln_residual(x.reshape(b * t, d), mix.reshape(b * t, d), ln1_g, ln1_b)
    return peer_block(h, ln2_g, ln2_b, peer_w).reshape(b, t, d)


def kernel(x_prompt, x_sample, cache_diff_k, cache_diff_v, cache_nsa_cmp_k, cache_nsa_cmp_v,
           cache_nsa_slc_k, cache_nsa_slc_v, state_nsa_win_k, state_nsa_win_v, page_table,
           w_in, cmp_w1_k, cmp_pe_k, cmp_w2_k, cmp_w1_v, cmp_pe_v, cmp_w2_v,
           lambda_q1, lambda_k1, lambda_q2, lambda_k2, diff_subln_g,
           w_br_nsa, w_br_diff, w_out, ln1_g, ln1_b,
           peer_wq, peer_subkeys, peer_u, peer_v, ln2_g, ln2_b):
    past_len = page_table.shape[1] * PAGE_SIZE
    seq, dec_seq = x_prompt.shape[1], x_sample.shape[1]
    assert x_prompt.shape[0] == 1 and dec_seq == 1 and w_in.shape[0] == 1
    pos_p = jnp.arange(seq, dtype=jnp.int32)
    pos_s = past_len + jnp.arange(dec_seq, dtype=jnp.int32)
    l = 0
    lam_init = 0.8 - 0.6 * math.exp(-0.3 * l)
    lam = (jnp.exp(jnp.sum(lambda_q1[l] * lambda_k1[l]))
           - jnp.exp(jnp.sum(lambda_q2[l] * lambda_k2[l])) + lam_init)
    cmp_k = (cmp_w1_k[l], cmp_pe_k[l], cmp_w2_k[l])
    cmp_v = (cmp_w1_v[l], cmp_pe_v[l], cmp_w2_v[l])
    peer_w = (peer_wq[l].astype(_BF16),
              peer_subkeys[l].reshape(PEER_HEADS * 2, N_KEYS, _PEER_HALF).astype(_BF16),
              peer_u[l].astype(_BF16), peer_v[l].astype(_BF16))

    wb = regroup_w_in(w_in[l])
    out_w = (w_br_nsa[l].astype(_BF16), w_br_diff[l].astype(_BF16), w_out[l].astype(_BF16))
    d = D_MODEL

    def kv_outputs(pj, nb, t):
        rows = lambda a: a.reshape(nb, t, G_NSA, DH_NSA)
        kf, vf = pj["kf"], pj["vf"]
        return (pj["k_d"].reshape(nb, t, H_DIFF, 2 * DH_DIFF), pj["v_d"].reshape(nb, t, H_DIFF, DV_DIFF),
                rows(kf[0]), rows(vf[0]), rows(kf[1]), rows(vf[1]), rows(kf[2]), rows(vf[2]))

    xp2 = x_prompt.reshape(seq, d)
    pj = mixer_project_pallas(xp2, pos_p, wb)
    o_n, o_d = prompt_attention(pj, cmp_k, cmp_v, lam, diff_subln_g[l], lam_init)
    hp = mixer_output_ln_pallas(o_n, o_d, pj["g_m"], xp2, *out_w, ln1_g[l], ln1_b[l])
    xp = peer_block(hp, ln2_g[l], ln2_b[l], peer_w).reshape(1, seq, d)
    kd, vd, kc, vc, ks, vs, kw, vw = kv_outputs(pj, 1, seq)
    keep = min(WINDOW, seq)
    outs_p = (kd, vd, kc, vc, ks, vs, kw[:, seq - keep:], vw[:, seq - keep:])

    nb = x_sample.shape[0]
    xs2 = x_sample.reshape(nb, d)
    pj = mixer_project_pallas(xs2, jnp.broadcast_to(pos_s, (nb,)), wb)
    kd, vd, kc, vc, ks, vs, kw, vw = kv_outputs(pj, nb, 1)
    q_n = pj["q_n"].astype(_F32).reshape(nb, 1, G_NSA, HPG_NSA, DH_NSA)
    g_n = pj["g_n"].reshape(nb, 1, H_NSA, 3)
    o_n, win_k_new, win_v_new = nsa_sample(q_n, kc, vc, ks, vs, kw, vw, g_n,
                                           cache_nsa_cmp_k, cache_nsa_cmp_v, cache_nsa_slc_k, cache_nsa_slc_v,
                                           state_nsa_win_k[l], state_nsa_win_v[l], page_table, l, cmp_k, cmp_v)
    q_d = pj["q_d"].astype(_F32).reshape(nb, H_DIFF, 2, DH_DIFF)
    o_d = diff_sample_pallas(q_d[:, :, 0], q_d[:, :, 1], kd[:, 0], vd[:, 0], cache_diff_k, cache_diff_v,
                             page_table, lam, diff_subln_g[l], lam_init)
    hs = mixer_output_ln_pallas(o_n.reshape(nb, -1).astype(_BF16), o_d.astype(_BF16), pj["g_m"], xs2,
                                *out_w, ln1_g[l], ln1_b[l])
    xs = peer_block(hs, ln2_g[l], ln2_b[l], peer_w).reshape(nb, 1, d)
    outs_s = (kd, vd, kc, vc, ks, vs, win_k_new, win_v_new)

    return (xp, xs) + tuple(a[None] for a in outs_p) + tuple(a[None] for a in outs_s)
```

```python
import functools
import math

import jax
import jax.numpy as jnp
from jax import lax
import numpy as np
from jax.experimental import pallas as pl
from jax.experimental.pallas import tpu as pltpu

D_MODEL = 2048
DEPTH = 1
PAGE_SIZE = 128
H_NSA = 16
G_NSA = 2
HPG_NSA = H_NSA // G_NSA
DH_NSA = 64
L_CMP = 32
STRIDE_CMP = 16
CMP_HID = 2 * DH_NSA
L_SLC = 64
N_SEL = 16
WINDOW = 512
H_DIFF = 8
DH_DIFF = 64
DV_DIFF = 2 * DH_DIFF
N_KEYS = 128
N_EXPERTS = N_KEYS * N_KEYS
PEER_HEADS = 8
PEER_DQ = 256
PEER_TOPK = 16
ROPE_THETA = 10000.0
Q_BLOCK = 128
LN_EPS = 1e-5
NEG_INF = -1e30
SEL_BONUS = 1e6
ALPHA = (2.0 * DEPTH) ** 0.25

W_NSA_Q = H_NSA * DH_NSA
W_NSA_KV = G_NSA * DH_NSA
W_NSA_GATE = H_NSA * 3
W_DIFF_QK = H_DIFF * 2 * DH_DIFF
W_DIFF_V = H_DIFF * DV_DIFF
COL_SIZES = (W_NSA_Q, W_NSA_KV, W_NSA_KV, W_NSA_KV, W_NSA_KV, W_NSA_KV, W_NSA_KV, W_NSA_GATE,
             W_DIFF_QK, W_DIFF_QK, W_DIFF_V, 2 * D_MODEL)


def layer_norm(x, g, b):
    xf = x.astype(jnp.float32)
    mu = xf.mean(-1, keepdims=True)
    var = jnp.square(xf - mu).mean(-1, keepdims=True)
    y = (xf - mu) * lax.rsqrt(var + LN_EPS) * g.astype(jnp.float32) + b.astype(jnp.float32)
    return y.astype(x.dtype)


def rope(x, pos):
    d = x.shape[-1]
    half = d // 2
    inv = ROPE_THETA ** (-jnp.arange(half, dtype=jnp.float32) / half)
    ang = pos.astype(jnp.float32)[:, None] * inv[None, :]
    cos, sin = jnp.cos(ang)[None, :, None, :], jnp.sin(ang)[None, :, None, :]
    xf = x.astype(jnp.float32)
    x1, x2 = xf[..., :half], xf[..., half:]
    return jnp.concatenate([x1 * cos - x2 * sin, x1 * sin + x2 * cos], -1).astype(x.dtype)


def masked_softmax(s, mask):
    s = jnp.where(mask, s.astype(jnp.float32), NEG_INF)
    return jnp.where(mask, jax.nn.softmax(s, axis=-1), 0.0)


def split_cols(z):
    out, start = [], 0
    for size in COL_SIZES:
        out.append(z[..., start:start + size])
        start += size
    return out


def gather_pages(cache, layer, page_table):
    b, n_pages = page_table.shape
    rows = cache[layer, page_table]
    return rows.reshape((b, n_pages * PAGE_SIZE) + cache.shape[3:])


def mixer_project(x, pos, w_in):
    b, t, _ = x.shape
    q_n, kc, vc, ks, vs, kw, vw, g_n, q_d, k_d, v_d, g_m = split_cols(x @ w_in)
    kvh = lambda a: a.reshape(b, t, G_NSA, DH_NSA)
    q_n = rope(q_n.reshape(b, t, H_NSA, DH_NSA), pos).reshape(b, t, G_NSA, HPG_NSA, DH_NSA)
    kc, ks, kw = rope(kvh(kc), pos), rope(kvh(ks), pos), rope(kvh(kw), pos)
    vc, vs, vw = kvh(vc), kvh(vs), kvh(vw)
    g_n = jax.nn.sigmoid(g_n.reshape(b, t, H_NSA, 3))
    q_d = q_d.reshape(b, t, H_DIFF, 2, DH_DIFF)
    k_d = k_d.reshape(b, t, H_DIFF, 2, DH_DIFF)
    q1, q2 = rope(q_d[:, :, :, 0], pos), rope(q_d[:, :, :, 1], pos)
    k_d = jnp.concatenate([rope(k_d[:, :, :, 0], pos), rope(k_d[:, :, :, 1], pos)], -1)
    v_d = v_d.reshape(b, t, H_DIFF, DV_DIFF)
    g_m = jax.nn.sigmoid(g_m.reshape(b, t, 2, D_MODEL))
    return (q_n, kc, vc, ks, vs, kw, vw, g_n, q1, q2, k_d, v_d, g_m)


def compress(rows, w1, pe, w2):
    b, t = rows.shape[:2]
    nc = (t - L_CMP) // STRIDE_CMP + 1
    r = rows[:, :STRIDE_CMP * (nc + 1)].reshape(b, nc + 1, STRIDE_CMP, G_NSA, DH_NSA)
    first = jnp.einsum('bnlgd,lde->bnge', r, w1[:STRIDE_CMP])
    second = jnp.einsum('bnlgd,lde->bnge', r, w1[STRIDE_CMP:])
    pe_term = jnp.einsum('ld,lde->e', pe, w1)
    hid = jax.nn.gelu(first[:, :-1] + second[:, 1:] + pe_term, approximate=False)
    return jnp.einsum('bnge,ed->bngd', hid, w2)


def nsa_attend(q, qpos, kc, vc, sel_fn, kw, vw, wpos, gates, n_slc):
    b, tq = q.shape[:2]
    nc = kc.shape[1]
    scale = DH_NSA ** -0.5
    t = qpos[:, None]
    cend = jnp.arange(nc, dtype=jnp.int32) * STRIDE_CMP + (L_CMP - 1)
    s = jnp.einsum('btghd,bngd->btghn', q, kc).astype(jnp.float32) * scale
    p_cmp = masked_softmax(s, (cend[None, :] <= t)[None, :, None, None, :])
    o_cmp = jnp.einsum('btghn,bngd->btghd', p_cmp.astype(vc.dtype), vc)
    ratio = L_SLC // STRIDE_CMP
    imp = jnp.pad(p_cmp.sum(axis=3), ((0, 0), (0, 0), (0, 0), (0, n_slc * ratio - nc)))
    imp = imp.reshape(b, tq, G_NSA, n_slc, ratio).sum(-1)
    blk = jnp.arange(n_slc, dtype=jnp.int32)[None, :]
    cur = (qpos // L_SLC)[:, None]
    forced = (blk == 0) | (blk == cur) | (blk == cur - 1)
    future = blk * L_SLC > t
    score = jnp.where(future[None, :, None, :], -SEL_BONUS,
                      imp + jnp.where(forced, SEL_BONUS, 0.0)[None, :, None, :])
    vals, idx = lax.top_k(score, min(N_SEL, n_slc))
    valid = vals > -0.5 * SEL_BONUS
    spos = idx[..., None] * L_SLC + jnp.arange(L_SLC, dtype=jnp.int32)
    ks, vs = sel_fn(spos)
    smask = (valid[..., None] & (spos <= qpos[None, :, None, None, None])).reshape(b, tq, G_NSA, -1)
    ks = ks.reshape(b, tq, G_NSA, -1, DH_NSA)
    vs = vs.reshape(b, tq, G_NSA, -1, DH_NSA)
    s = jnp.einsum('btghd,btgkd->btghk', q, ks).astype(jnp.float32) * scale
    p = masked_softmax(s, smask[:, :, :, None, :])
    o_slc = jnp.einsum('btghk,btgkd->btghd', p.astype(vs.dtype), vs)
    dist = t - wpos[None, :]
    wmask = (dist >= 0) & (dist <= WINDOW) & (wpos[None, :] >= 0)
    s = jnp.einsum('btghd,bsgd->btghs', q, kw).astype(jnp.float32) * scale
    p = masked_softmax(s, wmask[None, :, None, None, :])
    o_win = jnp.einsum('btghs,bsgd->btghd', p.astype(vw.dtype), vw)
    g = gates.reshape(b, tq, G_NSA, HPG_NSA, 3)
    o = g[..., 0:1] * o_cmp + g[..., 1:2] * o_slc + g[..., 2:3] * o_win
    return o.reshape(b, tq, H_NSA * DH_NSA)


def nsa_prompt(q, kc_rows, vc_rows, ks_rows, vs_rows, kw_rows, vw_rows, gates, cmp_k, cmp_v):
    b, t = q.shape[:2]
    kc, vc = compress(kc_rows, *cmp_k), compress(vc_rows, *cmp_v)
    n_slc = -(-t // L_SLC)
    bidx = jnp.arange(b)[:, None, None, None, None]
    gidx = jnp.arange(G_NSA)[None, None, :, None, None]

    def sel_fn(spos):
        p = jnp.clip(spos, 0, t - 1)
        return ks_rows[bidx, p, gidx], vs_rows[bidx, p, gidx]

    pad = ((0, 0), (WINDOW, 0), (0, 0), (0, 0))
    kw_pad, vw_pad = jnp.pad(kw_rows, pad), jnp.pad(vw_rows, pad)
    n_blk = t // Q_BLOCK
    q_blk = q.reshape(b, n_blk, Q_BLOCK, G_NSA, HPG_NSA, DH_NSA).swapaxes(0, 1)
    g_blk = gates.reshape(b, n_blk, Q_BLOCK, H_NSA, 3).swapaxes(0, 1)

    def one_block(args):
        i, qb, gb = args
        q0 = i * Q_BLOCK
        qpos = q0 + jnp.arange(Q_BLOCK, dtype=jnp.int32)
        wpos = q0 - WINDOW + jnp.arange(WINDOW + Q_BLOCK, dtype=jnp.int32)
        kw = lax.dynamic_slice_in_dim(kw_pad, q0, WINDOW + Q_BLOCK, axis=1)
        vw = lax.dynamic_slice_in_dim(vw_pad, q0, WINDOW + Q_BLOCK, axis=1)
        return nsa_attend(qb, qpos, kc, vc, sel_fn, kw, vw, wpos, gb, n_slc)

    out = lax.map(one_block, (jnp.arange(n_blk, dtype=jnp.int32), q_blk, g_blk))
    return out.swapaxes(0, 1).reshape(b, t, H_NSA * DH_NSA)


def nsa_sample(q, kc, vc, ks_new, vs_new, kw_new, vw_new, gates,
               cache_slc_k, cache_slc_v, win_k, win_v, page_table, layer):
    b, tn = q.shape[:2]
    past_len = page_table.shape[1] * PAGE_SIZE
    t_total = past_len + tn
    bidx = jnp.arange(b)[:, None, None, None, None]
    gidx = jnp.arange(G_NSA)[None, None, :, None, None]

    def sel_fn(spos):
        in_past = (spos < past_len)[..., None]
        pp = jnp.clip(spos, 0, past_len - 1)
        page = page_table[bidx, pp // PAGE_SIZE]
        off = pp % PAGE_SIZE
        pn = jnp.clip(spos - past_len, 0, tn - 1)
        k = jnp.where(in_past, cache_slc_k[layer, page, off, gidx], ks_new[bidx, pn, gidx])
        v = jnp.where(in_past, cache_slc_v[layer, page, off, gidx], vs_new[bidx, pn, gidx])
        return k, v

    wbuf = win_k.shape[1]
    kw = jnp.concatenate([win_k, kw_new], 1)
    vw = jnp.concatenate([win_v, vw_new], 1)
    wpos = past_len - wbuf + jnp.arange(wbuf + tn, dtype=jnp.int32)
    qpos = past_len + jnp.arange(tn, dtype=jnp.int32)
    o = nsa_attend(q, qpos, kc, vc, sel_fn, kw, vw, wpos, gates, -(-t_total // L_SLC))
    keep = min(WINDOW, t_total)
    return o, kw[:, wbuf + tn - keep:], vw[:, wbuf + tn - keep:]


def diff_attend(q1, q2, k, v, qpos, kpos, lam, subln_g, lam_init):
    b, tq = q1.shape[:2]
    scale = DH_DIFF ** -0.5
    mask = (kpos[None, :] <= qpos[:, None])[None, None]
    k1, k2 = k[..., :DH_DIFF], k[..., DH_DIFF:]
    a1 = masked_softmax(jnp.einsum('bthd,bshd->bhts', q1, k1).astype(jnp.float32) * scale, mask)
    a2 = masked_softmax(jnp.einsum('bthd,bshd->bhts', q2, k2).astype(jnp.float32) * scale, mask)
    o = jnp.einsum('bhts,bshd->bthd', (a1 - lam * a2).astype(v.dtype), v).astype(jnp.float32)
    o = o * lax.rsqrt(jnp.mean(jnp.square(o), -1, keepdims=True) + LN_EPS) * subln_g.astype(jnp.float32)
    return (o * (1.0 - lam_init)).astype(v.dtype).reshape(b, tq, H_DIFF * DV_DIFF)


def diff_prompt(q1, q2, k, v, lam, subln_g, lam_init):
    b, t = q1.shape[:2]
    n_blk = t // Q_BLOCK
    kpos = jnp.arange(t, dtype=jnp.int32)
    qb1 = q1.reshape(b, n_blk, Q_BLOCK, H_DIFF, DH_DIFF).swapaxes(0, 1)
    qb2 = q2.reshape(b, n_blk, Q_BLOCK, H_DIFF, DH_DIFF).swapaxes(0, 1)

    def one_block(args):
        i, a, c = args
        qpos = i * Q_BLOCK + jnp.arange(Q_BLOCK, dtype=jnp.int32)
        return diff_attend(a, c, k, v, qpos, kpos, lam, subln_g, lam_init)

    out = lax.map(one_block, (jnp.arange(n_blk, dtype=jnp.int32), qb1, qb2))
    return out.swapaxes(0, 1).reshape(b, t, H_DIFF * DV_DIFF)


def mixer_output(o_nsa, o_diff, g_m, w_br_nsa, w_br_diff, w_out):
    m = g_m[:, :, 0] * (o_nsa @ w_br_nsa) + g_m[:, :, 1] * (o_diff @ w_br_diff)
    return m @ w_out


def peer_ffn(xf, wq, subkeys, u, v):
    n = xf.shape[0]
    q = (xf @ wq).reshape(n, PEER_HEADS, 2, PEER_DQ // 2)
    s = jnp.einsum('nhpc,hpkc->nhpk', q, subkeys).astype(jnp.float32)
    s1, i1 = lax.top_k(s[:, :, 0], PEER_TOPK)
    s2, i2 = lax.top_k(s[:, :, 1], PEER_TOPK)
    cand = (s1[..., :, None] + s2[..., None, :]).reshape(n, PEER_HEADS, PEER_TOPK * PEER_TOPK)
    cid = (i1[..., :, None] * N_KEYS + i2[..., None, :]).reshape(n, PEER_HEADS, PEER_TOPK * PEER_TOPK)
    top, sel = lax.top_k(cand, PEER_TOPK)
    eid = jnp.take_along_axis(cid, sel, axis=-1)
    g = jax.nn.softmax(top, axis=-1)
    act = jax.nn.gelu(jnp.einsum('nd,nhkd->nhk', xf, u[eid]), approximate=False)
    return jnp.einsum('nhk,nhkd->nd', (g * act.astype(jnp.float32)).astype(xf.dtype), v[eid])


_BF16 = jnp.bfloat16
_F32 = jnp.float32
_VMEM_LIMIT = 56 * 1024 * 1024


def _dot_nt(a, b):
    return lax.dot_general(a, b, (((1,), (1,)), ((), ())), preferred_element_type=_F32)


def _dot(a, b):
    return jnp.dot(a, b, preferred_element_type=_F32)


def _dot_exact01(a, m01):
    hi = a.astype(_BF16)
    r1 = a - hi.astype(_F32)
    mid = r1.astype(_BF16)
    lo = (r1 - mid.astype(_F32)).astype(_BF16)
    return _dot(hi, m01) + _dot(mid, m01) + _dot(lo, m01)


def _gelu_erf(x):
    return 0.5 * x * (1.0 + lax.erf(x * (2.0 ** -0.5)))


def _softmax_rows(s, mask):
    s = jnp.where(mask, s, NEG_INF)
    m = jnp.max(s, axis=-1, keepdims=True)
    e = jnp.where(mask, jnp.exp(s - m), 0.0)
    l = jnp.sum(e, axis=-1, keepdims=True)
    return e / jnp.where(l > 0.0, l, 1.0)


_PJ_TN = 512
_PJ_TM = 1024
_OFF_QN, _OFF_QD, _OFF_KD, _OFF_VD = 0, W_NSA_Q, W_NSA_Q + W_DIFF_QK, W_NSA_Q + 2 * W_DIFF_QK
_OFF_GM = _OFF_VD + W_DIFF_V
_OFF_KV = _OFF_GM + 2 * D_MODEL
_OFF_GN = _OFF_KV + 6 * W_NSA_KV
_GN_PAD = 256
_PJ_COLS = _OFF_GN + _GN_PAD


def regroup_w_in(w):
    c = np.cumsum((0,) + COL_SIZES)
    seg = lambda i: w[:, c[i]:c[i + 1]]
    pad = jnp.zeros((w.shape[0], _GN_PAD - W_NSA_GATE), w.dtype)
    order = [seg(0), seg(8), seg(9), seg(10), seg(11)] + [seg(i) for i in range(1, 7)] + [seg(7), pad]
    return jnp.concatenate(order, axis=1).astype(_BF16)


def rope_tables(pos):
    half = DH_NSA // 2
    inv = ROPE_THETA ** (-jnp.arange(half, dtype=_F32) / half)
    ang = pos.astype(_F32)[:, None] * inv[None, :]
    cos, sin = jnp.cos(ang), jnp.sin(ang)
    return jnp.concatenate([cos, cos, cos, cos], 1), jnp.concatenate([-sin, sin, -sin, sin], 1)


def _rope_chunk(y, cos, sin_signed):
    lane = lax.broadcasted_iota(jnp.int32, y.shape, 1)
    partner = jnp.where(lane % DH_NSA < DH_NSA // 2, pltpu.roll(y, 128 - DH_NSA // 2, 1),
                        pltpu.roll(y, DH_NSA // 2, 1))
    return y * cos + partner * sin_signed


def _proj_kernel(x_ref, w_ref, cos_ref, sin_ref, *o_refs, kind):
    y = _dot(x_ref[...], w_ref[...])
    n_chunk = y.shape[1] // 128
    chunks = [y[:, c * 128:(c + 1) * 128] for c in range(n_chunk)]
    if kind in ("rope_bf16", "rope_f32_bf16"):
        cos, sin = cos_ref[...], sin_ref[...]
        chunks = [_rope_chunk(ch, cos, sin) for ch in chunks]
    if kind == "sigmoid_f32":
        o_refs[0][...] = jax.nn.sigmoid(y)
    elif kind == "rope_bf16":
        o_refs[0][...] = jnp.concatenate(chunks, axis=1).astype(_BF16)
    elif kind in ("rope_f32_bf16", "raw_f32_bf16"):
        z = jnp.concatenate(chunks, axis=1)
        o_refs[0][...] = z
        o_refs[1][...] = z.astype(_BF16)
    elif kind == "kv":
        kf_ref, vf_ref, kb_ref, vb_ref = o_refs
        cos, sin = cos_ref[...], sin_ref[...]
        for p in range(n_chunk // 2):
            kx = _rope_chunk(chunks[2 * p], cos, sin)
            vx = chunks[2 * p + 1]
            kf_ref[p] = kx
            vf_ref[p] = vx
            for g in range(G_NSA):
                kb_ref[p, g] = kx[:, g * DH_NSA:(g + 1) * DH_NSA].astype(_BF16)
                vb_ref[p, g] = vx[:, g * DH_NSA:(g + 1) * DH_NSA].astype(_BF16)


def _proj_call(xb, wb, cos, sin, kind, col_off, n_cols, tn, out_shapes, out_specs, name):
    t, d = xb.shape
    tm = min(_PJ_TM, t)
    off = col_off // tn
    return pl.pallas_call(
        functools.partial(_proj_kernel, kind=kind),
        grid=(n_cols // tn, t // tm),
        in_specs=[pl.BlockSpec((tm, d), lambda j, i: (i, 0)),
                  pl.BlockSpec((d, tn), lambda j, i: (0, off + j)),
                  pl.BlockSpec((tm, 128), lambda j, i: (i, 0)),
                  pl.BlockSpec((tm, 128), lambda j, i: (i, 0))],
        out_specs=out_specs(tm, tn),
        out_shape=out_shapes,
        compiler_params=pltpu.CompilerParams(
            dimension_semantics=("arbitrary", "arbitrary"), vmem_limit_bytes=_VMEM_LIMIT),
        name=name,
    )(xb, wb, cos, sin)


def mixer_project_pallas(x, pos, wb):
    t = x.shape[0]
    xb = x.astype(_BF16)
    cos, sin = rope_tables(pos)
    sds = jax.ShapeDtypeStruct
    plain = lambda tm, tn: pl.BlockSpec((tm, tn), lambda j, i: (i, j))
    one = lambda dt: lambda tm, tn: [plain(tm, tn)]
    two = lambda tm, tn: [plain(tm, tn), plain(tm, tn)]
    tn = _PJ_TN
    q_n, = _proj_call(xb, wb, cos, sin, "rope_bf16", _OFF_QN, W_NSA_Q, tn,
                      [sds((t, W_NSA_Q), _BF16)], one(_BF16), "proj_nsa_q")
    q_d, = _proj_call(xb, wb, cos, sin, "rope_bf16", _OFF_QD, W_DIFF_QK, tn,
                      [sds((t, W_DIFF_QK), _BF16)], one(_BF16), "proj_diff_q")
    k_d, k_db = _proj_call(xb, wb, cos, sin, "rope_f32_bf16", _OFF_KD, W_DIFF_QK, tn,
                           [sds((t, W_DIFF_QK), _F32), sds((t, W_DIFF_QK), _BF16)], two, "proj_diff_k")
    v_d, v_db = _proj_call(xb, wb, cos, sin, "raw_f32_bf16", _OFF_VD, W_DIFF_V, tn,
                           [sds((t, W_DIFF_V), _F32), sds((t, W_DIFF_V), _BF16)], two, "proj_diff_v")
    g_m, = _proj_call(xb, wb, cos, sin, "sigmoid_f32", _OFF_GM, 2 * D_MODEL, tn,
                      [sds((t, 2 * D_MODEL), _F32)], one(_F32), "proj_merge_gates")
    g_n, = _proj_call(xb, wb, cos, sin, "sigmoid_f32", _OFF_GN, _GN_PAD, _GN_PAD,
                      [sds((t, _GN_PAD), _F32)], one(_F32), "proj_nsa_gates")
    kv_tn = 2 * W_NSA_KV
    kv_specs = lambda tm, tn_: [
        pl.BlockSpec((1, tm, W_NSA_KV), lambda j, i: (j, i, 0)),
        pl.BlockSpec((1, tm, W_NSA_KV), lambda j, i: (j, i, 0)),
        pl.BlockSpec((1, G_NSA, tm, DH_NSA), lambda j, i: (j, 0, i, 0)),
        pl.BlockSpec((1, G_NSA, tm, DH_NSA), lambda j, i: (j, 0, i, 0))]
    kf, vf, kb, vb = _proj_call(
        xb, wb, cos, sin, "kv", _OFF_KV, 6 * W_NSA_KV, kv_tn,
        [sds((3, t, W_NSA_KV), _F32), sds((3, t, W_NSA_KV), _F32),
         sds((3, G_NSA, t, DH_NSA), _BF16), sds((3, G_NSA, t, DH_NSA), _BF16)], kv_specs, "proj_nsa_kv")
    return dict(q_n=q_n, q_d=q_d, k_d=k_d, k_db=k_db, v_d=v_d, v_db=v_db, g_m=g_m,
                g_n=g_n[:, :W_NSA_GATE], kf=kf, vf=vf, kb=kb, vb=vb)


def _merge_kernel(on_ref, od_ref, g0_ref, g1_ref, wn_ref, wd_ref, m_ref):
    m = g0_ref[...] * _dot(on_ref[...], wn_ref[...]) + g1_ref[...] * _dot(od_ref[...], wd_ref[...])
    m_ref[...] = m.astype(_BF16)


def _out_ln_kernel(m_ref, wo_ref, x_ref, g_ref, b_ref, h_ref):
    y = ALPHA * x_ref[...] + _dot(m_ref[...], wo_ref[...])
    mu = jnp.mean(y, axis=-1, keepdims=True)
    yc = y - mu
    var = jnp.mean(yc * yc, axis=-1, keepdims=True)
    h_ref[...] = yc * lax.rsqrt(var + LN_EPS) * g_ref[...] + b_ref[...]


def mixer_output_ln_pallas(o_n, o_d, g_m, x, wn, wd, wo, ln_g, ln_b):
    t, d = x.shape
    tm = min(_PJ_TM, t)
    tn = _PJ_TN
    nj = d // tn
    m = pl.pallas_call(
        _merge_kernel,
        grid=(nj, t // tm),
        in_specs=[pl.BlockSpec((tm, o_n.shape[1]), lambda j, i: (i, 0)),
                  pl.BlockSpec((tm, o_d.shape[1]), lambda j, i: (i, 0)),
                  pl.BlockSpec((tm, tn), lambda j, i: (i, j)),
                  pl.BlockSpec((tm, tn), lambda j, i: (i, nj + j)),
                  pl.BlockSpec((wn.shape[0], tn), lambda j, i: (0, j)),
                  pl.BlockSpec((wd.shape[0], tn), lambda j, i: (0, j))],
        out_specs=pl.BlockSpec((tm, tn), lambda j, i: (i, j)),
        out_shape=jax.ShapeDtypeStruct((t, d), _BF16),
        compiler_params=pltpu.CompilerParams(
            dimension_semantics=("arbitrary", "arbitrary"), vmem_limit_bytes=_VMEM_LIMIT),
        name="mixer_merge",
    )(o_n, o_d, g_m, g_m, wn, wd)
    tm2 = min(512, t)
    return pl.pallas_call(
        _out_ln_kernel,
        grid=(t // tm2,),
        in_specs=[pl.BlockSpec((tm2, d), lambda i: (i, 0)),
                  pl.BlockSpec((d, d), lambda i: (0, 0)),
                  pl.BlockSpec((tm2, d), lambda i: (i, 0)),
                  pl.BlockSpec((1, d), lambda i: (0, 0)),
                  pl.BlockSpec((1, d), lambda i: (0, 0))],
        out_specs=pl.BlockSpec((tm2, d), lambda i: (i, 0)),
        out_shape=jax.ShapeDtypeStruct((t, d), _F32),
        compiler_params=pltpu.CompilerParams(
            dimension_semantics=("arbitrary",), vmem_limit_bytes=_VMEM_LIMIT),
        name="mixer_out_ln",
    )(m, wo, x, ln_g.reshape(1, d), ln_b.reshape(1, d))


def _compress_kernel(r_ref, w1a_ref, w1b_ref, pe_ref, w1_ref, w2_ref, o_ref):
    r = r_ref[...]
    first = _dot(r, w1a_ref[...])
    second = _dot(r, w1b_ref[...])
    nb = first.shape[0]
    second_next = jnp.concatenate([second[1:], jnp.zeros((1, CMP_HID), _F32)], axis=0)
    pe_term = _dot(pe_ref[...], w1_ref[...])[0:1]
    hid = _gelu_erf(first + second_next + pe_term)
    o_ref[...] = _dot(hid.astype(_BF16), w2_ref[...])


def compress_pallas(rows, w1, pe, w2):
    t = rows.shape[0]
    nb = t // STRIDE_CMP
    r = rows.reshape(nb, STRIDE_CMP, G_NSA, DH_NSA).transpose(2, 0, 1, 3)
    r = r.reshape(G_NSA, nb, STRIDE_CMP * DH_NSA).astype(_BF16)
    w1f = w1.reshape(L_CMP * DH_NSA, CMP_HID).astype(_BF16)
    half = STRIDE_CMP * DH_NSA
    pe8 = jnp.broadcast_to(pe.reshape(1, L_CMP * DH_NSA), (8, L_CMP * DH_NSA)).astype(_BF16)
    full = lambda shape: pl.BlockSpec(shape, lambda g: (0,) * len(shape))
    return pl.pallas_call(
        _compress_kernel,
        grid=(G_NSA,),
        in_specs=[pl.BlockSpec((None, nb, half), lambda g: (g, 0, 0)),
                  full((half, CMP_HID)), full((half, CMP_HID)),
                  full((8, 2 * half)), full((2 * half, CMP_HID)), full((CMP_HID, DH_NSA))],
        out_specs=pl.BlockSpec((None, nb, DH_NSA), lambda g: (g, 0, 0)),
        out_shape=jax.ShapeDtypeStruct((G_NSA, nb, DH_NSA), _F32),
        name="nsa_compress",
    )(r, w1f[:half], w1f[half:], pe8, w1f, w2.astype(_BF16))


def _chunk_proj_kernel(x_ref, w_ref, o_ref):
    o_ref[...] = _dot(x_ref[...], w_ref[...])


def compress_paged_pallas(cache, page_table, w1, pe, w2, t_total):
    p = cache.shape[1]
    b, n_pages = page_table.shape
    cpp = PAGE_SIZE // STRIDE_CMP
    nc = (t_total - L_CMP) // STRIDE_CMP + 1
    assert (nc + 1) * STRIDE_CMP <= n_pages * PAGE_SIZE and L_CMP == 2 * STRIDE_CMP
    half = STRIDE_CMP * DH_NSA
    x = cache[0].reshape(p * cpp, STRIDE_CMP, G_NSA, DH_NSA).transpose(2, 0, 1, 3)
    x = x.reshape(G_NSA, p * cpp, half).astype(_BF16)
    w1f = w1.reshape(L_CMP * DH_NSA, CMP_HID).astype(_BF16)
    wcat = jnp.concatenate([w1f[:half], w1f[half:]], axis=1)
    rows = p * cpp
    tm = next(c for c in (2048, 1024, 512, 256, 128, 64, 32, 16, 8) if rows % c == 0)
    fs = pl.pallas_call(
        _chunk_proj_kernel,
        grid=(G_NSA, rows // tm),
        in_specs=[pl.BlockSpec((None, tm, half), lambda g, i: (g, i, 0)),
                  pl.BlockSpec((half, 2 * CMP_HID), lambda g, i: (0, 0))],
        out_specs=pl.BlockSpec((None, tm, 2 * CMP_HID), lambda g, i: (g, i, 0)),
        out_shape=jax.ShapeDtypeStruct((G_NSA, rows, 2 * CMP_HID), _F32),
        name="nsa_chunk_proj",
    )(x, wcat)
    fs = fs.reshape(G_NSA, p, cpp, 2 * CMP_HID)[:, page_table].reshape(G_NSA, b, n_pages * cpp, 2 * CMP_HID)
    pe_term = jnp.einsum('ld,lde->e', pe, w1)
    hid = jax.nn.gelu(fs[:, :, :nc, :CMP_HID] + fs[:, :, 1:nc + 1, CMP_HID:] + pe_term, approximate=False)
    return jnp.einsum('gbne,ed->bngd', hid, w2)


_NSA_TQ = 128
_NSA_TK = 1024


def _nsa_prompt_kernel(q_ref, kc_ref, vc_ref, ks_ref, vs_ref, kw_ref, vw_ref, g_ref, o_ref, st_ref,
                       *, nc, t_total):
    tq, tk, hh = _NSA_TQ, _NSA_TK, HPG_NSA
    q0 = pl.program_id(1) * tq
    qt = q_ref[...]
    q = jnp.concatenate([qt[:, h * DH_NSA:(h + 1) * DH_NSA] for h in range(hh)], axis=0)
    q = (q * (DH_NSA ** -0.5)).astype(_BF16)
    tpos = q0 + lax.broadcasted_iota(jnp.int32, (tq, 1), 0)

    ncp = kc_ref.shape[0]
    s = _dot_nt(q, kc_ref[...]).reshape(hh, tq, ncp)
    n_idx = lax.broadcasted_iota(jnp.int32, (tq, ncp), 1)
    cmask = ((n_idx * STRIDE_CMP + (L_CMP - 1)) <= tpos) & (n_idx < nc)
    p = _softmax_rows(s, cmask[None])
    o_cmp = _dot(p.reshape(hh * tq, ncp).astype(_BF16), vc_ref[...])
    ratio = L_SLC // STRIDE_CMP
    n_slc = ncp // ratio
    pool = (lax.broadcasted_iota(jnp.int32, (ncp, n_slc), 0) // ratio
            == lax.broadcasted_iota(jnp.int32, (ncp, n_slc), 1)).astype(_BF16)
    imp = _dot_exact01(jnp.sum(p, axis=0), pool)

    blk = lax.broadcasted_iota(jnp.int32, (tq, n_slc), 1)
    cur = tpos // L_SLC
    forced = (blk == 0) | (blk == cur) | (blk == cur - 1)
    future = blk * L_SLC > tpos
    score = jnp.where(future, -SEL_BONUS, imp + jnp.where(forced, SEL_BONUS, 0.0))
    st_ref[...] = score.T
    st = st_ref[...]
    b_idx = lax.broadcasted_iota(jnp.int32, (n_slc, tq), 0)

    def rank_body(bp, cnt):
        row = st_ref[pl.ds(bp, 1), :]
        ahead = (row > st) | ((row == st) & (b_idx > bp))
        return cnt + jnp.where(ahead, 1.0, 0.0)

    cnt = lax.fori_loop(0, n_slc, rank_body, jnp.zeros((n_slc, tq), _F32), unroll=8)
    sel_t = (cnt < float(N_SEL)) & (st > -0.5 * SEL_BONUS)
    sel = sel_t.astype(_F32).T.astype(_BF16)

    def slc_body(c, carry):
        m, l, acc = carry
        k0 = pl.multiple_of(c * tk, tk)
        kk = ks_ref[pl.ds(k0, tk), :]
        vv = vs_ref[pl.ds(k0, tk), :]
        sc = _dot_nt(q, kk).reshape(hh, tq, tk)
        kpos = k0 + lax.broadcasted_iota(jnp.int32, (1, tk), 1)
        expand = ((k0 + lax.broadcasted_iota(jnp.int32, (n_slc, tk), 1)) // L_SLC
                  == lax.broadcasted_iota(jnp.int32, (n_slc, tk), 0)).astype(_BF16)
        mk = (_dot(sel, expand) > 0.5) & (kpos <= tpos)
        mk = mk[None]
        sc = jnp.where(mk, sc, NEG_INF)
        m_new = jnp.maximum(m, jnp.max(sc, axis=-1, keepdims=True))
        alpha = jnp.exp(m - m_new)
        pp = jnp.exp(sc - m_new)
        l = alpha * l + jnp.sum(pp, axis=-1, keepdims=True)
        acc = acc * alpha.reshape(hh * tq, 1) + _dot(pp.reshape(hh * tq, tk).astype(_BF16), vv)
        return m_new, l, acc

    n_chunks = (q0 + tq + tk - 1) // tk
    m0 = jnp.full((hh, tq, 1), NEG_INF, _F32)
    l0 = jnp.zeros((hh, tq, 1), _F32)
    a0 = jnp.zeros((hh * tq, DH_NSA), _F32)
    _, l_s, acc_s = lax.fori_loop(0, n_chunks, slc_body, (m0, l0, a0))
    o_slc = acc_s / jnp.where(l_s > 0.0, l_s, 1.0).reshape(hh * tq, 1)

    wlen = WINDOW + tq
    w0 = pl.multiple_of(jnp.maximum(q0 - WINDOW, 0), tq)
    sw = _dot_nt(q, kw_ref[pl.ds(w0, wlen), :]).reshape(hh, tq, wlen)
    dist = tpos - (w0 + lax.broadcasted_iota(jnp.int32, (1, wlen), 1))
    wmask = (dist >= 0) & (dist <= WINDOW)
    pw = _softmax_rows(sw, wmask[None])
    o_win = _dot(pw.reshape(hh * tq, wlen).astype(_BF16), vw_ref[pl.ds(w0, wlen), :])

    g = g_ref[...]
    outs = []
    for h in range(hh):
        rows = slice(h * tq, (h + 1) * tq)
        outs.append(g[:, 3 * h:3 * h + 1] * o_cmp[rows] + g[:, 3 * h + 1:3 * h + 2] * o_slc[rows]
                    + g[:, 3 * h + 2:3 * h + 3] * o_win[rows])
    o_ref[...] = jnp.concatenate(outs, axis=1).astype(o_ref.dtype)


def nsa_prompt_pallas(q, kc, vc, ks_g, vs_g, kw_g, vw_g, gates, nc):
    t = q.shape[0]
    ncp = kc.shape[1]
    tq = _NSA_TQ
    assert t % _NSA_TK == 0 and t >= WINDOW + tq and ncp * STRIDE_CMP == t
    per_group = lambda a: a.astype(_BF16)
    ks_rows, vs_rows, kw_rows, vw_rows = ks_g, vs_g, kw_g, vw_g
    g3 = gates.reshape(t, G_NSA, HPG_NSA * 3).transpose(1, 0, 2)
    hw = HPG_NSA * DH_NSA
    kv_spec = pl.BlockSpec((None, t, DH_NSA), lambda g, i: (g, 0, 0))
    c_spec = pl.BlockSpec((None, ncp, DH_NSA), lambda g, i: (g, 0, 0))
    return pl.pallas_call(
        functools.partial(_nsa_prompt_kernel, nc=nc, t_total=t),
        grid=(G_NSA, t // tq),
        in_specs=[pl.BlockSpec((tq, hw), lambda g, i: (i, g)),
                  c_spec, c_spec, kv_spec, kv_spec, kv_spec, kv_spec,
                  pl.BlockSpec((None, tq, HPG_NSA * 3), lambda g, i: (g, i, 0))],
        out_specs=pl.BlockSpec((tq, hw), lambda g, i: (i, g)),
        out_shape=jax.ShapeDtypeStruct((t, H_NSA * DH_NSA), _BF16),
        scratch_shapes=[pltpu.VMEM((ncp // (L_SLC // STRIDE_CMP), tq), _F32)],
        compiler_params=pltpu.CompilerParams(
            dimension_semantics=("arbitrary", "arbitrary"), vmem_limit_bytes=_VMEM_LIMIT),
        name="nsa_prompt",
    )(q, kc.astype(_BF16), vc.astype(_BF16), per_group(ks_rows), per_group(vs_rows),
      per_group(kw_rows), per_group(vw_rows), g3)


_DIFF_TQ = 256
_DIFF_TK = 1024


def _diff_prompt_kernel(lam_ref, q_ref, k_ref, v_ref, gam_ref, o_ref, *, out_scale):
    tq, tk = _DIFF_TQ, _DIFF_TK
    q0 = pl.program_id(1) * tq
    qt = (q_ref[...] * (DH_DIFF ** -0.5)).astype(_BF16)
    q1, q2 = qt[:, :DH_DIFF], qt[:, DH_DIFF:]
    tpos = q0 + lax.broadcasted_iota(jnp.int32, (tq, 1), 0)

    def make_body(masked):
        def body(c, carry):
            m1, l1, a1, m2, l2, a2 = carry
            k0 = pl.multiple_of(c * tk, tk)
            kk = k_ref[pl.ds(k0, tk), :]
            vv = v_ref[pl.ds(k0, tk), :]
            if masked:
                mk = (k0 + lax.broadcasted_iota(jnp.int32, (1, tk), 1)) <= tpos

            def stream(qh, kh, m, l, a):
                sc = _dot_nt(qh, kh)
                if masked:
                    sc = jnp.where(mk, sc, NEG_INF)
                m_new = jnp.maximum(m, jnp.max(sc, axis=-1, keepdims=True))
                alpha = jnp.exp(m - m_new)
                pp = jnp.exp(sc - m_new)
                l = alpha * l + jnp.sum(pp, axis=-1, keepdims=True)
                a = alpha * a + _dot(pp.astype(_BF16), vv)
                return m_new, l, a

            m1, l1, a1 = stream(q1, kk[:, :DH_DIFF], m1, l1, a1)
            m2, l2, a2 = stream(q2, kk[:, DH_DIFF:], m2, l2, a2)
            return m1, l1, a1, m2, l2, a2
        return body

    n_chunks = (q0 + tq + tk - 1) // tk
    mi = jnp.full((tq, 1), NEG_INF, _F32)
    li = jnp.zeros((tq, 1), _F32)
    ai = jnp.zeros((tq, DV_DIFF), _F32)
    carry = lax.fori_loop(0, n_chunks - 1, make_body(False), (mi, li, ai, mi, li, ai))
    _, l1, a1, _, l2, a2 = make_body(True)(n_chunks - 1, carry)
    lam = lam_ref[0:1, 0:1]
    o = a1 / l1 - lam * (a2 / l2)
    o = o * lax.rsqrt(jnp.mean(o * o, axis=-1, keepdims=True) + LN_EPS) * gam_ref[...]
    o_ref[...] = (o * out_scale).astype(o_ref.dtype)


def diff_prompt_pallas(q, k, v, lam, subln_g, lam_init):
    t = q.shape[0]
    tq = _DIFF_TQ
    assert t % _DIFF_TK == 0 and _DIFF_TK % _DIFF_TQ == 0
    lam_row = jnp.full((8, 128), lam, _F32)
    hw = 2 * DH_DIFF
    return pl.pallas_call(
        functools.partial(_diff_prompt_kernel, out_scale=1.0 - lam_init),
        grid=(H_DIFF, t // tq),
        in_specs=[pl.BlockSpec((8, 128), lambda h, i: (0, 0)),
                  pl.BlockSpec((tq, hw), lambda h, i: (i, h)),
                  pl.BlockSpec((t, hw), lambda h, i: (0, h)),
                  pl.BlockSpec((t, DV_DIFF), lambda h, i: (0, h)),
                  pl.BlockSpec((1, DV_DIFF), lambda h, i: (0, 0))],
        out_specs=pl.BlockSpec((tq, DV_DIFF), lambda h, i: (i, h)),
        out_shape=jax.ShapeDtypeStruct((t, H_DIFF * DV_DIFF), _BF16),
        compiler_params=pltpu.CompilerParams(
            dimension_semantics=("arbitrary", "arbitrary"), vmem_limit_bytes=_VMEM_LIMIT),
        name="diff_prompt",
    )(lam_row, q, k.astype(_BF16), v.astype(_BF16), subln_g.reshape(1, DV_DIFF))


_SD_PAGES = 8
_SD_STREAMS = 2 * H_DIFF


def _diff_sample_kernel(pt_ref, lam_ref, q_ref, kn_ref, vn_ref, gam_ref, *refs, out_scale):
    k_refs, v_refs = refs[:_SD_PAGES], refs[_SD_PAGES:2 * _SD_PAGES]
    o_ref, m_ref, l_ref, acc_ref = refs[2 * _SD_PAGES:]
    c = pl.program_id(1)
    q = q_ref[...]
    own_head = lambda n: (lax.broadcasted_iota(jnp.int32, (_SD_STREAMS, n), 1) % H_DIFF
                          == lax.broadcasted_iota(jnp.int32, (_SD_STREAMS, n), 0) // 2)

    @pl.when(c == 0)
    def _():
        s0 = _dot_nt(q, kn_ref[...].astype(_BF16))
        m_ref[...] = jnp.max(jnp.where(own_head(H_DIFF), s0, NEG_INF), axis=-1, keepdims=True)
        l_ref[...] = jnp.ones(l_ref.shape, _F32)
        acc_ref[...] = vn_ref[...].astype(_BF16).astype(_F32)

    rows = PAGE_SIZE * H_DIFF
    flat = lambda r: r[...].reshape(rows, 2 * DH_DIFF).astype(_BF16)
    kk = jnp.concatenate([flat(r) for r in k_refs], axis=0)
    vv = jnp.concatenate([flat(r) for r in v_refs], axis=0)
    mk = own_head(_SD_PAGES * rows)
    sc = jnp.where(mk, _dot_nt(q, kk), NEG_INF)
    m_old = m_ref[...]
    m_new = jnp.maximum(m_old, jnp.max(sc, axis=-1, keepdims=True))
    alpha = jnp.exp(m_old - m_new)
    pp = jnp.where(mk, jnp.exp(sc - m_new), 0.0)
    m_ref[...] = m_new
    l_ref[...] = alpha * l_ref[...] + jnp.sum(pp, axis=-1, keepdims=True)
    acc_ref[...] = alpha * acc_ref[...] + _dot(pp.astype(_BF16), vv)

    @pl.when(c == pl.num_programs(1) - 1)
    def _():
        lam = lam_ref[0:1, 0:1]
        on = acc_ref[...] / l_ref[...]
        outs = []
        for h in range(H_DIFF):
            o = on[2 * h:2 * h + 1, :] - lam * on[2 * h + 1:2 * h + 2, :]
            o = o * lax.rsqrt(jnp.mean(o * o, axis=-1, keepdims=True) + LN_EPS) * gam_ref[...]
            outs.append(o * out_scale)
        o_ref[...] = jnp.concatenate(outs, axis=1)


def diff_sample_pallas(q1, q2, k_new, v_new, cache_k, cache_v, page_table, lam, subln_g, lam_init):
    b, n_pages = page_table.shape
    hw = 2 * DH_DIFF
    assert n_pages % _SD_PAGES == 0 and DV_DIFF == hw
    qs = jnp.stack([q1, q2], axis=2) * (DH_DIFF ** -0.5)
    q16 = jnp.einsum('bhjd,ji->bhjid', qs, jnp.eye(2, dtype=_F32)).reshape(b, _SD_STREAMS, hw).astype(_BF16)
    vn16 = jnp.repeat(v_new, 2, axis=1)
    lam_row = jnp.full((8, 128), lam, _F32)
    page_spec = lambda j: pl.BlockSpec((None, None, PAGE_SIZE, H_DIFF, hw),
                                       lambda bb, c, pt: (0, pt[bb, c * _SD_PAGES + j], 0, 0, 0))
    per_b = lambda rows: pl.BlockSpec((None, rows, hw), lambda bb, c, pt: (bb, 0, 0))
    grid_spec = pltpu.PrefetchScalarGridSpec(
        num_scalar_prefetch=1,
        grid=(b, n_pages // _SD_PAGES),
        in_specs=[pl.BlockSpec((8, 128), lambda bb, c, pt: (0, 0)),
                  per_b(_SD_STREAMS), per_b(H_DIFF), per_b(_SD_STREAMS),
                  pl.BlockSpec((1, DV_DIFF), lambda bb, c, pt: (0, 0))]
                 + [page_spec(j) for j in range(_SD_PAGES)] * 2,
        out_specs=pl.BlockSpec((None, 1, H_DIFF * DV_DIFF), lambda bb, c, pt: (bb, 0, 0)),
        scratch_shapes=[pltpu.VMEM((_SD_STREAMS, 1), _F32), pltpu.VMEM((_SD_STREAMS, 1), _F32),
                        pltpu.VMEM((_SD_STREAMS, DV_DIFF), _F32)],
    )
    out = pl.pallas_call(
        functools.partial(_diff_sample_kernel, out_scale=1.0 - lam_init),
        grid_spec=grid_spec,
        out_shape=jax.ShapeDtypeStruct((b, 1, H_DIFF * DV_DIFF), _F32),
        compiler_params=pltpu.CompilerParams(
            dimension_semantics=("arbitrary", "arbitrary"), vmem_limit_bytes=_VMEM_LIMIT),
        name="diff_sample",
    )(page_table, lam_row, q16, k_new, vn16, subln_g.reshape(1, DV_DIFF),
      *([cache_k] * _SD_PAGES), *([cache_v] * _SD_PAGES))
    return out.reshape(b, H_DIFF * DV_DIFF)


_PEER_PAIRS = tuple((x, y) for x in range(PEER_TOPK) for y in range(PEER_TOPK)
                    if (x + 1) * (y + 1) <= PEER_TOPK)
_PEER_NPAIR_PAD = -(-len(_PEER_PAIRS) // 8) * 8
_PEER_HALF = PEER_DQ // 2


def _distinct_top(x):
    vals, cnts = [], []
    for _ in range(PEER_TOPK):
        m = jnp.max(x, axis=0, keepdims=True)
        eq = x == m
        vals.append(m)
        cnts.append(jnp.sum(jnp.where(eq, 1.0, 0.0), axis=0, keepdims=True))
        x = jnp.where(eq, NEG_INF, x)
    return vals, cnts


def _peer_a_kernel(h_ref, wq_ref, sk_ref, s1_ref, e1_ref, s2_ref, e2_ref, tau_ref, c_ref, w_ref):
    tn = h_ref.shape[0]
    q = _dot(h_ref[...], wq_ref[...]).astype(_BF16)
    cols = lambda hh, p: q[:, (2 * hh + p) * _PEER_HALF:(2 * hh + p + 1) * _PEER_HALF]
    s1 = jnp.concatenate([_dot_nt(sk_ref[2 * hh], cols(hh, 0)) for hh in range(PEER_HEADS)], axis=1)
    s2 = jnp.concatenate([_dot_nt(sk_ref[2 * hh + 1], cols(hh, 1)) for hh in range(PEER_HEADS)], axis=1)
    a, ca = _distinct_top(s1)
    b, cb = _distinct_top(s2)
    c_ref[...] = jnp.full(c_ref.shape, NEG_INF, _F32)
    w_ref[...] = jnp.zeros(w_ref.shape, _F32)
    for k, (x, y) in enumerate(_PEER_PAIRS):
        c_ref[k:k + 1, :] = a[x] + b[y]
        w_ref[k:k + 1, :] = ca[x] * cb[y]
    cand, wgt = c_ref[...], w_ref[...]
    tau = jnp.full(a[0].shape, NEG_INF, _F32)
    for k in range(len(_PEER_PAIRS)):
        cu = c_ref[k:k + 1, :]
        n_ge = jnp.sum(jnp.where(cand >= cu, wgt, 0.0), axis=0, keepdims=True)
        tau = jnp.maximum(tau, jnp.where(n_ge >= float(PEER_TOPK), cu, NEG_INF))
    top = a[0] + b[0]
    z = jnp.sum(jnp.where(cand >= tau, wgt * jnp.exp(cand - top), 0.0), axis=0, keepdims=True)
    e1 = jnp.exp(s1 - a[0])
    e2 = jnp.exp(s2 - b[0]) / z
    for hh in range(PEER_HEADS):
        lanes = slice(hh * tn, (hh + 1) * tn)
        s1_ref[hh] = s1[:, lanes]
        e1_ref[hh] = e1[:, lanes]
        s2_ref[hh] = s2[:, lanes]
        e2_ref[hh] = e2[:, lanes]
        tau_ref[hh:hh + 1, :] = tau[:, lanes]


_PEER_RB = 32
_PEER_PIECE = 256


def _peer_b_kernel(h_ref, s1_ref, e1_ref, s2_ref, e2_ref, tau_ref, u_ref, v_ref, o_ref, g_ref):
    tn = h_ref.shape[0]
    c = pl.program_id(1)
    n_sub = u_ref.shape[0] // N_KEYS
    s1_rows = [[s1_ref[hh, pl.ds(c * n_sub + ii, 1), :] for ii in range(n_sub)] for hh in range(PEER_HEADS)]
    e1_rows = [[e1_ref[hh, pl.ds(c * n_sub + ii, 1), :] for ii in range(n_sub)] for hh in range(PEER_HEADS)]
    w_parts = []
    pw = min(_PEER_PIECE, tn)
    for piece in range(tn // pw):
        tok = slice(piece * pw, (piece + 1) * pw)
        a_t = _dot_nt(u_ref[...], h_ref[tok, :])
        for cb in range(piece * pw // 128, (piece + 1) * pw // 128):
            lanes = slice(cb * 128, (cb + 1) * 128)
            for rb in range(N_KEYS // _PEER_RB):
                rows = slice(rb * _PEER_RB, (rb + 1) * _PEER_RB)
                accs = [jnp.zeros((_PEER_RB, 128), _F32) for _ in range(n_sub)]
                for hh in range(PEER_HEADS):
                    s2 = s2_ref[hh, rows, lanes]
                    e2 = e2_ref[hh, rows, lanes]
                    tau = tau_ref[hh:hh + 1, lanes]
                    for ii in range(n_sub):
                        pair = s2 + s1_rows[hh][ii][:, lanes]
                        accs[ii] = accs[ii] + jnp.where(pair >= tau, e2, 0.0) * e1_rows[hh][ii][:, lanes]
                for ii in range(n_sub):
                    g_ref[ii * N_KEYS + rb * _PEER_RB:ii * N_KEYS + (rb + 1) * _PEER_RB, lanes] = accs[ii]
        w_parts.append((g_ref[:, tok] * _gelu_erf(a_t)).astype(_BF16))
    w_t = jnp.concatenate(w_parts, axis=1)
    contrib = lax.dot_general(w_t, v_ref[...], (((0,), (0,)), ((), ())), preferred_element_type=_F32)

    @pl.when(c == 0)
    def _():
        o_ref[...] = contrib

    @pl.when(c > 0)
    def _():
        o_ref[...] += contrib


_PEER_E_CHUNK = 1024
_PEER_TN = 512


def peer_ffn_pallas(h, wq, subkeys, u, v, *, tn):
    n, d = h.shape
    assert n % tn == 0
    hb = h.astype(_BF16)
    r = PEER_HEADS * tn
    stat = jax.ShapeDtypeStruct((PEER_HEADS, N_KEYS, n), _F32)
    stat_spec = pl.BlockSpec((PEER_HEADS, N_KEYS, tn), lambda i: (0, 0, i))
    s1, e1, s2, e2, tau = pl.pallas_call(
        _peer_a_kernel,
        grid=(n // tn,),
        in_specs=[pl.BlockSpec((tn, d), lambda i: (i, 0)),
                  pl.BlockSpec(wq.shape, lambda i: (0, 0)),
                  pl.BlockSpec(subkeys.shape, lambda i: (0, 0, 0))],
        out_specs=[stat_spec, stat_spec, stat_spec, stat_spec,
                   pl.BlockSpec((PEER_HEADS, tn), lambda i: (0, i))],
        out_shape=[stat, stat, stat, stat, jax.ShapeDtypeStruct((PEER_HEADS, n), _F32)],
        scratch_shapes=[pltpu.VMEM((_PEER_NPAIR_PAD, r), _F32), pltpu.VMEM((_PEER_NPAIR_PAD, r), _F32)],
        compiler_params=pltpu.CompilerParams(
            dimension_semantics=("arbitrary",), vmem_limit_bytes=_VMEM_LIMIT),
        name="peer_scores",
    )(hb, wq, subkeys)
    e = _PEER_E_CHUNK
    stat_spec_b = pl.BlockSpec((PEER_HEADS, N_KEYS, tn), lambda i, c: (0, 0, i))
    return pl.pallas_call(
        _peer_b_kernel,
        grid=(n // tn, N_EXPERTS // e),
        in_specs=[pl.BlockSpec((tn, d), lambda i, c: (i, 0)),
                  stat_spec_b, stat_spec_b, stat_spec_b, stat_spec_b,
                  pl.BlockSpec((PEER_HEADS, tn), lambda i, c: (0, i)),
                  pl.BlockSpec((e, d), lambda i, c: (c, 0)),
                  pl.BlockSpec((e, d), lambda i, c: (c, 0))],
        out_specs=pl.BlockSpec((tn, d), lambda i, c: (i, 0)),
        out_shape=jax.ShapeDtypeStruct((n, d), _F32),
        scratch_shapes=[pltpu.VMEM((e, tn), _F32)],
        compiler_params=pltpu.CompilerParams(
            dimension_semantics=("arbitrary", "arbitrary"), vmem_limit_bytes=_VMEM_LIMIT),
        name="peer_experts",
    )(hb, s1, e1, s2, e2, tau, u, v)


def _ln_residual_kernel(a_ref, f_ref, g_ref, b_ref, o_ref):
    x = ALPHA * a_ref[...] + f_ref[...]
    mu = jnp.mean(x, axis=-1, keepdims=True)
    xc = x - mu
    var = jnp.mean(xc * xc, axis=-1, keepdims=True)
    o_ref[...] = xc * lax.rsqrt(var + LN_EPS) * g_ref[...] + b_ref[...]


def ln_residual(a, f, g, b, *, tm=256):
    n, d = a.shape
    tm = min(tm, n)
    return pl.pallas_call(
        _ln_residual_kernel,
        grid=(n // tm,),
        in_specs=[pl.BlockSpec((tm, d), lambda i: (i, 0)),
                  pl.BlockSpec((tm, d), lambda i: (i, 0)),
                  pl.BlockSpec((1, d), lambda i: (0, 0)),
                  pl.BlockSpec((1, d), lambda i: (0, 0))],
        out_specs=pl.BlockSpec((tm, d), lambda i: (i, 0)),
        out_shape=jax.ShapeDtypeStruct((n, d), jnp.float32),
        name="ln_residual",
    )(a, f, g.reshape(1, d), b.reshape(1, d))


def prompt_attention(pj, cmp_k, cmp_v, lam, subln_g, lam_init):
    t = pj["q_n"].shape[0]
    nc = (t - L_CMP) // STRIDE_CMP + 1
    kc_c = compress_pallas(pj["kf"][0], *cmp_k)
    vc_c = compress_pallas(pj["vf"][0], *cmp_v)
    kb, vb = pj["kb"], pj["vb"]
    o_n = nsa_prompt_pallas(pj["q_n"], kc_c, vc_c, kb[1], vb[1], kb[2], vb[2], pj["g_n"], nc)
    o_d = diff_prompt_pallas(pj["q_d"], pj["k_db"], pj["v_db"], lam, subln_g, lam_init)
    return o_n, o_d


def peer_block(h, ln2_g, ln2_b, peer_w):
    n = h.shape[0]
    tn = min(_PEER_TN, -(-n // Q_BLOCK) * Q_BLOCK)
    n_pad = -(-n // tn) * tn
    f = peer_ffn_pallas(jnp.pad(h, ((0, n_pad - n), (0, 0))), *peer_w, tn=tn)[:n]
    return ln_residual(h, f, ln2_g, ln2_b)


def post_block(x, mix, ln1_g, ln1_b, ln2_g, ln2_b, peer_w):
    b, t, d = x.shape
    h = ln_residual(x.reshape(b * t, d), mix.reshape(b * t, d), ln1_g, ln1_b)
    return peer_block(h, ln2_g, ln2_b, peer_w).reshape(b, t, d)


def kernel(x_prompt, x_sample, cache_diff_k, cache_diff_v, cache_nsa_cmp_k, cache_nsa_cmp_v,
           cache_nsa_slc_k, cache_nsa_slc_v, state_nsa_win_k, state_nsa_win_v, page_table,
           w_in, cmp_w1_k, cmp_pe_k, cmp_w2_k, cmp_w1_v, cmp_pe_v, cmp_w2_v,
           lambda_q1, lambda_k1, lambda_q2, lambda_k2, diff_subln_g,
           w_br_nsa, w_br_diff, w_out, ln1_g, ln1_b,
           peer_wq, peer_subkeys, peer_u, peer_v, ln2_g, ln2_b):
    past_len = page_table.shape[1] * PAGE_SIZE
    seq, dec_seq = x_prompt.shape[1], x_sample.shape[1]
    assert x_prompt.shape[0] == 1 and dec_seq == 1 and w_in.shape[0] == 1
    pos_p = jnp.arange(seq, dtype=jnp.int32)
    pos_s = past_len + jnp.arange(dec_seq, dtype=jnp.int32)
    l = 0
    lam_init = 0.8 - 0.6 * math.exp(-0.3 * l)
    lam = (jnp.exp(jnp.sum(lambda_q1[l] * lambda_k1[l]))
           - jnp.exp(jnp.sum(lambda_q2[l] * lambda_k2[l])) + lam_init)
    cmp_k = (cmp_w1_k[l], cmp_pe_k[l], cmp_w2_k[l])
    cmp_v = (cmp_w1_v[l], cmp_pe_v[l], cmp_w2_v[l])
    peer_w = (peer_wq[l].astype(_BF16),
              peer_subkeys[l].reshape(PEER_HEADS * 2, N_KEYS, _PEER_HALF).astype(_BF16),
              peer_u[l].astype(_BF16), peer_v[l].astype(_BF16))

    wb = regroup_w_in(w_in[l])
    out_w = (w_br_nsa[l].astype(_BF16), w_br_diff[l].astype(_BF16), w_out[l].astype(_BF16))
    d = D_MODEL

    def kv_outputs(pj, nb, t):
        rows = lambda a: a.reshape(nb, t, G_NSA, DH_NSA)
        kf, vf = pj["kf"], pj["vf"]
        return (pj["k_d"].reshape(nb, t, H_DIFF, 2 * DH_DIFF), pj["v_d"].reshape(nb, t, H_DIFF, DV_DIFF),
                rows(kf[0]), rows(vf[0]), rows(kf[1]), rows(vf[1]), rows(kf[2]), rows(vf[2]))

    xp2 = x_prompt.reshape(seq, d)
    pj = mixer_project_pallas(xp2, pos_p, wb)
    o_n, o_d = prompt_attention(pj, cmp_k, cmp_v, lam, diff_subln_g[l], lam_init)
    hp = mixer_output_ln_pallas(o_n, o_d, pj["g_m"], xp2, *out_w, ln1_g[l], ln1_b[l])
    xp = peer_block(hp, ln2_g[l], ln2_b[l], peer_w).reshape(1, seq, d)
    kd, vd, kc, vc, ks, vs, kw, vw = kv_outputs(pj, 1, seq)
    keep = min(WINDOW, seq)
    outs_p = (kd, vd, kc, vc, ks, vs, kw[:, seq - keep:], vw[:, seq - keep:])

    nb = x_sample.shape[0]
    xs2 = x_sample.reshape(nb, d)
    pj = mixer_project_pallas(xs2, jnp.broadcast_to(pos_s, (nb,)), wb)
    kd, vd, kc, vc, ks, vs, kw, vw = kv_outputs(pj, nb, 1)
    q_n = pj["q_n"].astype(_F32).reshape(nb, 1, G_NSA, HPG_NSA, DH_NSA)
    g_n = pj["g_n"].reshape(nb, 1, H_NSA, 3)
    t_total = past_len + dec_seq
    kc_c = compress_paged_pallas(cache_nsa_cmp_k, page_table, *cmp_k, t_total)
    vc_c = compress_paged_pallas(cache_nsa_cmp_v, page_table, *cmp_v, t_total)
    o_n, win_k_new, win_v_new = nsa_sample(q_n, kc_c, vc_c, ks, vs, kw, vw, g_n,
                                           cache_nsa_slc_k, cache_nsa_slc_v,
                                           state_nsa_win_k[l], state_nsa_win_v[l], page_table, l)
    q_d = pj["q_d"].astype(_F32).reshape(nb, H_DIFF, 2, DH_DIFF)
    o_d = diff_sample_pallas(q_d[:, :, 0], q_d[:, :, 1], kd[:, 0], vd[:, 0], cache_diff_k, cache_diff_v,
                             page_table, lam, diff_subln_g[l], lam_init)
    hs = mixer_output_ln_pallas(o_n.reshape(nb, -1).astype(_BF16), o_d.astype(_BF16), pj["g_m"], xs2,
                                *out_w, ln1_g[l], ln1_b[l])
    xs = peer_block(hs, ln2_g[l], ln2_b[l], peer_w).reshape(nb, 1, d)
    outs_s = (kd, vd, kc, vc, ks, vs, win_k_new, win_v_new)

    return (xp, xs) + tuple(a[None] for a in outs_p) + tuple(a[None] for a in outs_s)
```

```python
import functools
import math

import jax
import jax.numpy as jnp
from jax import lax
import numpy as np
from jax.experimental import pallas as pl
from jax.experimental.pallas import tpu as pltpu

D_MODEL = 2048
DEPTH = 1
PAGE_SIZE = 128
H_NSA = 16
G_NSA = 2
HPG_NSA = H_NSA // G_NSA
DH_NSA = 64
L_CMP = 32
STRIDE_CMP = 16
CMP_HID = 2 * DH_NSA
L_SLC = 64
N_SEL = 16
WINDOW = 512
H_DIFF = 8
DH_DIFF = 64
DV_DIFF = 2 * DH_DIFF
N_KEYS = 128
N_EXPERTS = N_KEYS * N_KEYS
PEER_HEADS = 8
PEER_DQ = 256
PEER_TOPK = 16
ROPE_THETA = 10000.0
Q_BLOCK = 128
LN_EPS = 1e-5
NEG_INF = -1e30
SEL_BONUS = 1e6
ALPHA = (2.0 * DEPTH) ** 0.25

W_NSA_Q = H_NSA * DH_NSA
W_NSA_KV = G_NSA * DH_NSA
W_NSA_GATE = H_NSA * 3
W_DIFF_QK = H_DIFF * 2 * DH_DIFF
W_DIFF_V = H_DIFF * DV_DIFF
COL_SIZES = (W_NSA_Q, W_NSA_KV, W_NSA_KV, W_NSA_KV, W_NSA_KV, W_NSA_KV, W_NSA_KV, W_NSA_GATE,
             W_DIFF_QK, W_DIFF_QK, W_DIFF_V, 2 * D_MODEL)


def layer_norm(x, g, b):
    xf = x.astype(jnp.float32)
    mu = xf.mean(-1, keepdims=True)
    var = jnp.square(xf - mu).mean(-1, keepdims=True)
    y = (xf - mu) * lax.rsqrt(var + LN_EPS) * g.astype(jnp.float32) + b.astype(jnp.float32)
    return y.astype(x.dtype)


def rope(x, pos):
    d = x.shape[-1]
    half = d // 2
    inv = ROPE_THETA ** (-jnp.arange(half, dtype=jnp.float32) / half)
    ang = pos.astype(jnp.float32)[:, None] * inv[None, :]
    cos, sin = jnp.cos(ang)[None, :, None, :], jnp.sin(ang)[None, :, None, :]
    xf = x.astype(jnp.float32)
    x1, x2 = xf[..., :half], xf[..., half:]
    return jnp.concatenate([x1 * cos - x2 * sin, x1 * sin + x2 * cos], -1).astype(x.dtype)


def masked_softmax(s, mask):
    s = jnp.where(mask, s.astype(jnp.float32), NEG_INF)
    return jnp.where(mask, jax.nn.softmax(s, axis=-1), 0.0)


def split_cols(z):
    out, start = [], 0
    for size in COL_SIZES:
        out.append(z[..., start:start + size])
        start += size
    return out


def gather_pages(cache, layer, page_table):
    b, n_pages = page_table.shape
    rows = cache[layer, page_table]
    return rows.reshape((b, n_pages * PAGE_SIZE) + cache.shape[3:])


def mixer_project(x, pos, w_in):
    b, t, _ = x.shape
    q_n, kc, vc, ks, vs, kw, vw, g_n, q_d, k_d, v_d, g_m = split_cols(x @ w_in)
    kvh = lambda a: a.reshape(b, t, G_NSA, DH_NSA)
    q_n = rope(q_n.reshape(b, t, H_NSA, DH_NSA), pos).reshape(b, t, G_NSA, HPG_NSA, DH_NSA)
    kc, ks, kw = rope(kvh(kc), pos), rope(kvh(ks), pos), rope(kvh(kw), pos)
    vc, vs, vw = kvh(vc), kvh(vs), kvh(vw)
    g_n = jax.nn.sigmoid(g_n.reshape(b, t, H_NSA, 3))
    q_d = q_d.reshape(b, t, H_DIFF, 2, DH_DIFF)
    k_d = k_d.reshape(b, t, H_DIFF, 2, DH_DIFF)
    q1, q2 = rope(q_d[:, :, :, 0], pos), rope(q_d[:, :, :, 1], pos)
    k_d = jnp.concatenate([rope(k_d[:, :, :, 0], pos), rope(k_d[:, :, :, 1], pos)], -1)
    v_d = v_d.reshape(b, t, H_DIFF, DV_DIFF)
    g_m = jax.nn.sigmoid(g_m.reshape(b, t, 2, D_MODEL))
    return (q_n, kc, vc, ks, vs, kw, vw, g_n, q1, q2, k_d, v_d, g_m)


def compress(rows, w1, pe, w2):
    b, t = rows.shape[:2]
    nc = (t - L_CMP) // STRIDE_CMP + 1
    r = rows[:, :STRIDE_CMP * (nc + 1)].reshape(b, nc + 1, STRIDE_CMP, G_NSA, DH_NSA)
    first = jnp.einsum('bnlgd,lde->bnge', r, w1[:STRIDE_CMP])
    second = jnp.einsum('bnlgd,lde->bnge', r, w1[STRIDE_CMP:])
    pe_term = jnp.einsum('ld,lde->e', pe, w1)
    hid = jax.nn.gelu(first[:, :-1] + second[:, 1:] + pe_term, approximate=False)
    return jnp.einsum('bnge,ed->bngd', hid, w2)


def nsa_attend(q, qpos, kc, vc, sel_fn, kw, vw, wpos, gates, n_slc):
    b, tq = q.shape[:2]
    nc = kc.shape[1]
    scale = DH_NSA ** -0.5
    t = qpos[:, None]
    cend = jnp.arange(nc, dtype=jnp.int32) * STRIDE_CMP + (L_CMP - 1)
    s = jnp.einsum('btghd,bngd->btghn', q, kc).astype(jnp.float32) * scale
    p_cmp = masked_softmax(s, (cend[None, :] <= t)[None, :, None, None, :])
    o_cmp = jnp.einsum('btghn,bngd->btghd', p_cmp.astype(vc.dtype), vc)
    ratio = L_SLC // STRIDE_CMP
    imp = jnp.pad(p_cmp.sum(axis=3), ((0, 0), (0, 0), (0, 0), (0, n_slc * ratio - nc)))
    imp = imp.reshape(b, tq, G_NSA, n_slc, ratio).sum(-1)
    blk = jnp.arange(n_slc, dtype=jnp.int32)[None, :]
    cur = (qpos // L_SLC)[:, None]
    forced = (blk == 0) | (blk == cur) | (blk == cur - 1)
    future = blk * L_SLC > t
    score = jnp.where(future[None, :, None, :], -SEL_BONUS,
                      imp + jnp.where(forced, SEL_BONUS, 0.0)[None, :, None, :])
    vals, idx = lax.top_k(score, min(N_SEL, n_slc))
    valid = vals > -0.5 * SEL_BONUS
    spos = idx[..., None] * L_SLC + jnp.arange(L_SLC, dtype=jnp.int32)
    ks, vs = sel_fn(spos)
    smask = (valid[..., None] & (spos <= qpos[None, :, None, None, None])).reshape(b, tq, G_NSA, -1)
    ks = ks.reshape(b, tq, G_NSA, -1, DH_NSA)
    vs = vs.reshape(b, tq, G_NSA, -1, DH_NSA)
    s = jnp.einsum('btghd,btgkd->btghk', q, ks).astype(jnp.float32) * scale
    p = masked_softmax(s, smask[:, :, :, None, :])
    o_slc = jnp.einsum('btghk,btgkd->btghd', p.astype(vs.dtype), vs)
    dist = t - wpos[None, :]
    wmask = (dist >= 0) & (dist <= WINDOW) & (wpos[None, :] >= 0)
    s = jnp.einsum('btghd,bsgd->btghs', q, kw).astype(jnp.float32) * scale
    p = masked_softmax(s, wmask[None, :, None, None, :])
    o_win = jnp.einsum('btghs,bsgd->btghd', p.astype(vw.dtype), vw)
    g = gates.reshape(b, tq, G_NSA, HPG_NSA, 3)
    o = g[..., 0:1] * o_cmp + g[..., 1:2] * o_slc + g[..., 2:3] * o_win
    return o.reshape(b, tq, H_NSA * DH_NSA)


def nsa_prompt(q, kc_rows, vc_rows, ks_rows, vs_rows, kw_rows, vw_rows, gates, cmp_k, cmp_v):
    b, t = q.shape[:2]
    kc, vc = compress(kc_rows, *cmp_k), compress(vc_rows, *cmp_v)
    n_slc = -(-t // L_SLC)
    bidx = jnp.arange(b)[:, None, None, None, None]
    gidx = jnp.arange(G_NSA)[None, None, :, None, None]

    def sel_fn(spos):
        p = jnp.clip(spos, 0, t - 1)
        return ks_rows[bidx, p, gidx], vs_rows[bidx, p, gidx]

    pad = ((0, 0), (WINDOW, 0), (0, 0), (0, 0))
    kw_pad, vw_pad = jnp.pad(kw_rows, pad), jnp.pad(vw_rows, pad)
    n_blk = t // Q_BLOCK
    q_blk = q.reshape(b, n_blk, Q_BLOCK, G_NSA, HPG_NSA, DH_NSA).swapaxes(0, 1)
    g_blk = gates.reshape(b, n_blk, Q_BLOCK, H_NSA, 3).swapaxes(0, 1)

    def one_block(args):
        i, qb, gb = args
        q0 = i * Q_BLOCK
        qpos = q0 + jnp.arange(Q_BLOCK, dtype=jnp.int32)
        wpos = q0 - WINDOW + jnp.arange(WINDOW + Q_BLOCK, dtype=jnp.int32)
        kw = lax.dynamic_slice_in_dim(kw_pad, q0, WINDOW + Q_BLOCK, axis=1)
        vw = lax.dynamic_slice_in_dim(vw_pad, q0, WINDOW + Q_BLOCK, axis=1)
        return nsa_attend(qb, qpos, kc, vc, sel_fn, kw, vw, wpos, gb, n_slc)

    out = lax.map(one_block, (jnp.arange(n_blk, dtype=jnp.int32), q_blk, g_blk))
    return out.swapaxes(0, 1).reshape(b, t, H_NSA * DH_NSA)


def nsa_sample(q, kc, vc, ks_new, vs_new, kw_new, vw_new, gates,
               cache_slc_k, cache_slc_v, win_k, win_v, page_table, layer):
    b, tn = q.shape[:2]
    past_len = page_table.shape[1] * PAGE_SIZE
    t_total = past_len + tn
    bidx = jnp.arange(b)[:, None, None, None, None]
    gidx = jnp.arange(G_NSA)[None, None, :, None, None]

    assert PAGE_SIZE % L_SLC == 0 and past_len % L_SLC == 0 and tn == 1
    bpp = PAGE_SIZE // L_SLC
    n_pool = cache_slc_k.shape[1]

    def sel_fn(spos):
        blk = spos[..., 0] // L_SLC
        in_past = (blk * L_SLC < past_len)[..., None, None]
        pblk = jnp.clip(blk, 0, past_len // L_SLC - 1)
        slot = page_table[bidx[..., 0], pblk // bpp] * bpp + pblk % bpp

        def rows(cache, new):
            blocks = cache[layer].reshape(n_pool * bpp, L_SLC, G_NSA, DH_NSA)[slot]
            own = jnp.stack([blocks[:, :, g, :, :, g, :] for g in range(G_NSA)], axis=2)
            return jnp.where(in_past, own, new[:, 0][:, None, :, None, None, :])

        return rows(cache_slc_k, ks_new), rows(cache_slc_v, vs_new)

    wbuf = win_k.shape[1]
    kw = jnp.concatenate([win_k, kw_new], 1)
    vw = jnp.concatenate([win_v, vw_new], 1)
    wpos = past_len - wbuf + jnp.arange(wbuf + tn, dtype=jnp.int32)
    qpos = past_len + jnp.arange(tn, dtype=jnp.int32)
    o = nsa_attend(q, qpos, kc, vc, sel_fn, kw, vw, wpos, gates, -(-t_total // L_SLC))
    keep = min(WINDOW, t_total)
    return o, kw[:, wbuf + tn - keep:], vw[:, wbuf + tn - keep:]


def diff_attend(q1, q2, k, v, qpos, kpos, lam, subln_g, lam_init):
    b, tq = q1.shape[:2]
    scale = DH_DIFF ** -0.5
    mask = (kpos[None, :] <= qpos[:, None])[None, None]
    k1, k2 = k[..., :DH_DIFF], k[..., DH_DIFF:]
    a1 = masked_softmax(jnp.einsum('bthd,bshd->bhts', q1, k1).astype(jnp.float32) * scale, mask)
    a2 = masked_softmax(jnp.einsum('bthd,bshd->bhts', q2, k2).astype(jnp.float32) * scale, mask)
    o = jnp.einsum('bhts,bshd->bthd', (a1 - lam * a2).astype(v.dtype), v).astype(jnp.float32)
    o = o * lax.rsqrt(jnp.mean(jnp.square(o), -1, keepdims=True) + LN_EPS) * subln_g.astype(jnp.float32)
    return (o * (1.0 - lam_init)).astype(v.dtype).reshape(b, tq, H_DIFF * DV_DIFF)


def diff_prompt(q1, q2, k, v, lam, subln_g, lam_init):
    b, t = q1.shape[:2]
    n_blk = t // Q_BLOCK
    kpos = jnp.arange(t, dtype=jnp.int32)
    qb1 = q1.reshape(b, n_blk, Q_BLOCK, H_DIFF, DH_DIFF).swapaxes(0, 1)
    qb2 = q2.reshape(b, n_blk, Q_BLOCK, H_DIFF, DH_DIFF).swapaxes(0, 1)

    def one_block(args):
        i, a, c = args
        qpos = i * Q_BLOCK + jnp.arange(Q_BLOCK, dtype=jnp.int32)
        return diff_attend(a, c, k, v, qpos, kpos, lam, subln_g, lam_init)

    out = lax.map(one_block, (jnp.arange(n_blk, dtype=jnp.int32), qb1, qb2))
    return out.swapaxes(0, 1).reshape(b, t, H_DIFF * DV_DIFF)


def mixer_output(o_nsa, o_diff, g_m, w_br_nsa, w_br_diff, w_out):
    m = g_m[:, :, 0] * (o_nsa @ w_br_nsa) + g_m[:, :, 1] * (o_diff @ w_br_diff)
    return m @ w_out


def peer_ffn(xf, wq, subkeys, u, v):
    n = xf.shape[0]
    q = (xf @ wq).reshape(n, PEER_HEADS, 2, PEER_DQ // 2)
    s = jnp.einsum('nhpc,hpkc->nhpk', q, subkeys).astype(jnp.float32)
    s1, i1 = lax.top_k(s[:, :, 0], PEER_TOPK)
    s2, i2 = lax.top_k(s[:, :, 1], PEER_TOPK)
    cand = (s1[..., :, None] + s2[..., None, :]).reshape(n, PEER_HEADS, PEER_TOPK * PEER_TOPK)
    cid = (i1[..., :, None] * N_KEYS + i2[..., None, :]).reshape(n, PEER_HEADS, PEER_TOPK * PEER_TOPK)
    top, sel = lax.top_k(cand, PEER_TOPK)
    eid = jnp.take_along_axis(cid, sel, axis=-1)
    g = jax.nn.softmax(top, axis=-1)
    act = jax.nn.gelu(jnp.einsum('nd,nhkd->nhk', xf, u[eid]), approximate=False)
    return jnp.einsum('nhk,nhkd->nd', (g * act.astype(jnp.float32)).astype(xf.dtype), v[eid])


_BF16 = jnp.bfloat16
_F32 = jnp.float32
_VMEM_LIMIT = 56 * 1024 * 1024


def _dot_nt(a, b):
    return lax.dot_general(a, b, (((1,), (1,)), ((), ())), preferred_element_type=_F32)


def _dot(a, b):
    return jnp.dot(a, b, preferred_element_type=_F32)


def _dot_exact01(a, m01):
    hi = a.astype(_BF16)
    r1 = a - hi.astype(_F32)
    mid = r1.astype(_BF16)
    lo = (r1 - mid.astype(_F32)).astype(_BF16)
    return _dot(hi, m01) + _dot(mid, m01) + _dot(lo, m01)


def _gelu_erf(x):
    return 0.5 * x * (1.0 + lax.erf(x * (2.0 ** -0.5)))


def _softmax_rows(s, mask):
    s = jnp.where(mask, s, NEG_INF)
    m = jnp.max(s, axis=-1, keepdims=True)
    e = jnp.where(mask, jnp.exp(s - m), 0.0)
    l = jnp.sum(e, axis=-1, keepdims=True)
    return e / jnp.where(l > 0.0, l, 1.0)


_PJ_TN = 512
_PJ_TM = 1024
_OFF_QN, _OFF_QD, _OFF_KD, _OFF_VD = 0, W_NSA_Q, W_NSA_Q + W_DIFF_QK, W_NSA_Q + 2 * W_DIFF_QK
_OFF_GM = _OFF_VD + W_DIFF_V
_OFF_KV = _OFF_GM + 2 * D_MODEL
_OFF_GN = _OFF_KV + 6 * W_NSA_KV
_GN_PAD = 256
_PJ_COLS = _OFF_GN + _GN_PAD


def regroup_w_in(w):
    c = np.cumsum((0,) + COL_SIZES)
    seg = lambda i: w[:, c[i]:c[i + 1]]
    pad = jnp.zeros((w.shape[0], _GN_PAD - W_NSA_GATE), w.dtype)
    order = [seg(0), seg(8), seg(9), seg(10), seg(11)] + [seg(i) for i in range(1, 7)] + [seg(7), pad]
    return jnp.concatenate(order, axis=1).astype(_BF16)


def rope_tables(pos):
    half = DH_NSA // 2
    inv = ROPE_THETA ** (-jnp.arange(half, dtype=_F32) / half)
    ang = pos.astype(_F32)[:, None] * inv[None, :]
    cos, sin = jnp.cos(ang), jnp.sin(ang)
    return jnp.concatenate([cos, cos, cos, cos], 1), jnp.concatenate([-sin, sin, -sin, sin], 1)


def _rope_chunk(y, cos, sin_signed):
    lane = lax.broadcasted_iota(jnp.int32, y.shape, 1)
    partner = jnp.where(lane % DH_NSA < DH_NSA // 2, pltpu.roll(y, 128 - DH_NSA // 2, 1),
                        pltpu.roll(y, DH_NSA // 2, 1))
    return y * cos + partner * sin_signed


def _proj_kernel(x_ref, w_ref, cos_ref, sin_ref, *o_refs, kind):
    y = _dot(x_ref[...], w_ref[...])
    n_chunk = y.shape[1] // 128
    chunks = [y[:, c * 128:(c + 1) * 128] for c in range(n_chunk)]
    if kind in ("rope_bf16", "rope_f32_bf16"):
        cos, sin = cos_ref[...], sin_ref[...]
        chunks = [_rope_chunk(ch, cos, sin) for ch in chunks]
    if kind == "sigmoid_f32":
        o_refs[0][...] = jax.nn.sigmoid(y)
    elif kind == "rope_bf16":
        o_refs[0][...] = jnp.concatenate(chunks, axis=1).astype(_BF16)
    elif kind in ("rope_f32_bf16", "raw_f32_bf16"):
        z = jnp.concatenate(chunks, axis=1)
        o_refs[0][...] = z
        o_refs[1][...] = z.astype(_BF16)
    elif kind == "kv":
        kf_ref, vf_ref, kb_ref, vb_ref = o_refs
        cos, sin = cos_ref[...], sin_ref[...]
        for p in range(n_chunk // 2):
            kx = _rope_chunk(chunks[2 * p], cos, sin)
            vx = chunks[2 * p + 1]
            kf_ref[p] = kx
            vf_ref[p] = vx
            for g in range(G_NSA):
                kb_ref[p, g] = kx[:, g * DH_NSA:(g + 1) * DH_NSA].astype(_BF16)
                vb_ref[p, g] = vx[:, g * DH_NSA:(g + 1) * DH_NSA].astype(_BF16)


def _proj_call(xb, wb, cos, sin, kind, col_off, n_cols, tn, out_shapes, out_specs, name):
    t, d = xb.shape
    tm = min(_PJ_TM, t)
    off = col_off // tn
    return pl.pallas_call(
        functools.partial(_proj_kernel, kind=kind),
        grid=(n_cols // tn, t // tm),
        in_specs=[pl.BlockSpec((tm, d), lambda j, i: (i, 0)),
                  pl.BlockSpec((d, tn), lambda j, i: (0, off + j)),
                  pl.BlockSpec((tm, 128), lambda j, i: (i, 0)),
                  pl.BlockSpec((tm, 128), lambda j, i: (i, 0))],
        out_specs=out_specs(tm, tn),
        out_shape=out_shapes,
        compiler_params=pltpu.CompilerParams(
            dimension_semantics=("arbitrary", "arbitrary"), vmem_limit_bytes=_VMEM_LIMIT),
        name=name,
    )(xb, wb, cos, sin)


def mixer_project_pallas(x, pos, wb):
    t = x.shape[0]
    xb = x.astype(_BF16)
    cos, sin = rope_tables(pos)
    sds = jax.ShapeDtypeStruct
    plain = lambda tm, tn: pl.BlockSpec((tm, tn), lambda j, i: (i, j))
    one = lambda dt: lambda tm, tn: [plain(tm, tn)]
    two = lambda tm, tn: [plain(tm, tn), plain(tm, tn)]
    tn = _PJ_TN
    q_n, = _proj_call(xb, wb, cos, sin, "rope_bf16", _OFF_QN, W_NSA_Q, tn,
                      [sds((t, W_NSA_Q), _BF16)], one(_BF16), "proj_nsa_q")
    q_d, = _proj_call(xb, wb, cos, sin, "rope_bf16", _OFF_QD, W_DIFF_QK, tn,
                      [sds((t, W_DIFF_QK), _BF16)], one(_BF16), "proj_diff_q")
    k_d, k_db = _proj_call(xb, wb, cos, sin, "rope_f32_bf16", _OFF_KD, W_DIFF_QK, tn,
                           [sds((t, W_DIFF_QK), _F32), sds((t, W_DIFF_QK), _BF16)], two, "proj_diff_k")
    v_d, v_db = _proj_call(xb, wb, cos, sin, "raw_f32_bf16", _OFF_VD, W_DIFF_V, tn,
                           [sds((t, W_DIFF_V), _F32), sds((t, W_DIFF_V), _BF16)], two, "proj_diff_v")
    g_m, = _proj_call(xb, wb, cos, sin, "sigmoid_f32", _OFF_GM, 2 * D_MODEL, tn,
                      [sds((t, 2 * D_MODEL), _F32)], one(_F32), "proj_merge_gates")
    g_n, = _proj_call(xb, wb, cos, sin, "sigmoid_f32", _OFF_GN, _GN_PAD, _GN_PAD,
                      [sds((t, _GN_PAD), _F32)], one(_F32), "proj_nsa_gates")
    kv_tn = 2 * W_NSA_KV
    kv_specs = lambda tm, tn_: [
        pl.BlockSpec((1, tm, W_NSA_KV), lambda j, i: (j, i, 0)),
        pl.BlockSpec((1, tm, W_NSA_KV), lambda j, i: (j, i, 0)),
        pl.BlockSpec((1, G_NSA, tm, DH_NSA), lambda j, i: (j, 0, i, 0)),
        pl.BlockSpec((1, G_NSA, tm, DH_NSA), lambda j, i: (j, 0, i, 0))]
    kf, vf, kb, vb = _proj_call(
        xb, wb, cos, sin, "kv", _OFF_KV, 6 * W_NSA_KV, kv_tn,
        [sds((3, t, W_NSA_KV), _F32), sds((3, t, W_NSA_KV), _F32),
         sds((3, G_NSA, t, DH_NSA), _BF16), sds((3, G_NSA, t, DH_NSA), _BF16)], kv_specs, "proj_nsa_kv")
    return dict(q_n=q_n, q_d=q_d, k_d=k_d, k_db=k_db, v_d=v_d, v_db=v_db, g_m=g_m,
                g_n=g_n[:, :W_NSA_GATE], kf=kf, vf=vf, kb=kb, vb=vb)


def _merge_kernel(on_ref, od_ref, g0_ref, g1_ref, wn_ref, wd_ref, m_ref):
    m = g0_ref[...] * _dot(on_ref[...], wn_ref[...]) + g1_ref[...] * _dot(od_ref[...], wd_ref[...])
    m_ref[...] = m.astype(_BF16)


def _out_ln_kernel(m_ref, wo_ref, x_ref, g_ref, b_ref, h_ref):
    y = ALPHA * x_ref[...] + _dot(m_ref[...], wo_ref[...])
    mu = jnp.mean(y, axis=-1, keepdims=True)
    yc = y - mu
    var = jnp.mean(yc * yc, axis=-1, keepdims=True)
    h_ref[...] = yc * lax.rsqrt(var + LN_EPS) * g_ref[...] + b_ref[...]


def mixer_output_ln_pallas(o_n, o_d, g_m, x, wn, wd, wo, ln_g, ln_b):
    t, d = x.shape
    tm = min(_PJ_TM, t)
    tn = _PJ_TN
    nj = d // tn
    m = pl.pallas_call(
        _merge_kernel,
        grid=(nj, t // tm),
        in_specs=[pl.BlockSpec((tm, o_n.shape[1]), lambda j, i: (i, 0)),
                  pl.BlockSpec((tm, o_d.shape[1]), lambda j, i: (i, 0)),
                  pl.BlockSpec((tm, tn), lambda j, i: (i, j)),
                  pl.BlockSpec((tm, tn), lambda j, i: (i, nj + j)),
                  pl.BlockSpec((wn.shape[0], tn), lambda j, i: (0, j)),
                  pl.BlockSpec((wd.shape[0], tn), lambda j, i: (0, j))],
        out_specs=pl.BlockSpec((tm, tn), lambda j, i: (i, j)),
        out_shape=jax.ShapeDtypeStruct((t, d), _BF16),
        compiler_params=pltpu.CompilerParams(
            dimension_semantics=("arbitrary", "arbitrary"), vmem_limit_bytes=_VMEM_LIMIT),
        name="mixer_merge",
    )(o_n, o_d, g_m, g_m, wn, wd)
    tm2 = min(512, t)
    return pl.pallas_call(
        _out_ln_kernel,
        grid=(t // tm2,),
        in_specs=[pl.BlockSpec((tm2, d), lambda i: (i, 0)),
                  pl.BlockSpec((d, d), lambda i: (0, 0)),
                  pl.BlockSpec((tm2, d), lambda i: (i, 0)),
                  pl.BlockSpec((1, d), lambda i: (0, 0)),
                  pl.BlockSpec((1, d), lambda i: (0, 0))],
        out_specs=pl.BlockSpec((tm2, d), lambda i: (i, 0)),
        out_shape=jax.ShapeDtypeStruct((t, d), _F32),
        compiler_params=pltpu.CompilerParams(
            dimension_semantics=("arbitrary",), vmem_limit_bytes=_VMEM_LIMIT),
        name="mixer_out_ln",
    )(m, wo, x, ln_g.reshape(1, d), ln_b.reshape(1, d))


def _compress_kernel(r_ref, w1a_ref, w1b_ref, pe_ref, w1_ref, w2_ref, o_ref):
    r = r_ref[...]
    first = _dot(r, w1a_ref[...])
    second = _dot(r, w1b_ref[...])
    nb = first.shape[0]
    second_next = jnp.concatenate([second[1:], jnp.zeros((1, CMP_HID), _F32)], axis=0)
    pe_term = _dot(pe_ref[...], w1_ref[...])[0:1]
    hid = _gelu_erf(first + second_next + pe_term)
    o_ref[...] = _dot(hid.astype(_BF16), w2_ref[...])


def compress_pallas(rows, w1, pe, w2):
    t = rows.shape[0]
    nb = t // STRIDE_CMP
    r = rows.reshape(nb, STRIDE_CMP, G_NSA, DH_NSA).transpose(2, 0, 1, 3)
    r = r.reshape(G_NSA, nb, STRIDE_CMP * DH_NSA).astype(_BF16)
    w1f = w1.reshape(L_CMP * DH_NSA, CMP_HID).astype(_BF16)
    half = STRIDE_CMP * DH_NSA
    pe8 = jnp.broadcast_to(pe.reshape(1, L_CMP * DH_NSA), (8, L_CMP * DH_NSA)).astype(_BF16)
    full = lambda shape: pl.BlockSpec(shape, lambda g: (0,) * len(shape))
    return pl.pallas_call(
        _compress_kernel,
        grid=(G_NSA,),
        in_specs=[pl.BlockSpec((None, nb, half), lambda g: (g, 0, 0)),
                  full((half, CMP_HID)), full((half, CMP_HID)),
                  full((8, 2 * half)), full((2 * half, CMP_HID)), full((CMP_HID, DH_NSA))],
        out_specs=pl.BlockSpec((None, nb, DH_NSA), lambda g: (g, 0, 0)),
        out_shape=jax.ShapeDtypeStruct((G_NSA, nb, DH_NSA), _F32),
        name="nsa_compress",
    )(r, w1f[:half], w1f[half:], pe8, w1f, w2.astype(_BF16))


def _chunk_proj_kernel(x_ref, w_ref, o_ref):
    o_ref[...] = _dot(x_ref[...], w_ref[...])


def compress_paged_pallas(cache, page_table, w1, pe, w2, t_total):
    p = cache.shape[1]
    b, n_pages = page_table.shape
    cpp = PAGE_SIZE // STRIDE_CMP
    nc = (t_total - L_CMP) // STRIDE_CMP + 1
    assert (nc + 1) * STRIDE_CMP <= n_pages * PAGE_SIZE and L_CMP == 2 * STRIDE_CMP
    half = STRIDE_CMP * DH_NSA
    x = cache[0].reshape(p * cpp, STRIDE_CMP, G_NSA, DH_NSA).transpose(2, 0, 1, 3)
    x = x.reshape(G_NSA, p * cpp, half).astype(_BF16)
    w1f = w1.reshape(L_CMP * DH_NSA, CMP_HID).astype(_BF16)
    wcat = jnp.concatenate([w1f[:half], w1f[half:]], axis=1)
    rows = p * cpp
    tm = next(c for c in (2048, 1024, 512, 256, 128, 64, 32, 16, 8) if rows % c == 0)
    fs = pl.pallas_call(
        _chunk_proj_kernel,
        grid=(G_NSA, rows // tm),
        in_specs=[pl.BlockSpec((None, tm, half), lambda g, i: (g, i, 0)),
                  pl.BlockSpec((half, 2 * CMP_HID), lambda g, i: (0, 0))],
        out_specs=pl.BlockSpec((None, tm, 2 * CMP_HID), lambda g, i: (g, i, 0)),
        out_shape=jax.ShapeDtypeStruct((G_NSA, rows, 2 * CMP_HID), _F32),
        name="nsa_chunk_proj",
    )(x, wcat)
    fs = fs.reshape(G_NSA, p, cpp, 2 * CMP_HID)[:, page_table].reshape(G_NSA, b, n_pages * cpp, 2 * CMP_HID)
    pe_term = jnp.einsum('ld,lde->e', pe, w1)
    hid = jax.nn.gelu(fs[:, :, :nc, :CMP_HID] + fs[:, :, 1:nc + 1, CMP_HID:] + pe_term, approximate=False)
    return jnp.einsum('gbne,ed->bngd', hid, w2)


_NSA_TQ = 128
_NSA_TK = 1024


def _nsa_prompt_kernel(q_ref, kc_ref, vc_ref, ks_ref, vs_ref, kw_ref, vw_ref, g_ref, o_ref, st_ref,
                       *, nc, t_total):
    tq, tk, hh = _NSA_TQ, _NSA_TK, HPG_NSA
    q0 = pl.program_id(1) * tq
    qt = q_ref[...]
    q = jnp.concatenate([qt[:, h * DH_NSA:(h + 1) * DH_NSA] for h in range(hh)], axis=0)
    q = (q * (DH_NSA ** -0.5)).astype(_BF16)
    tpos = q0 + lax.broadcasted_iota(jnp.int32, (tq, 1), 0)

    ncp = kc_ref.shape[0]
    s = _dot_nt(q, kc_ref[...]).reshape(hh, tq, ncp)
    n_idx = lax.broadcasted_iota(jnp.int32, (tq, ncp), 1)
    cmask = ((n_idx * STRIDE_CMP + (L_CMP - 1)) <= tpos) & (n_idx < nc)
    p = _softmax_rows(s, cmask[None])
    o_cmp = _dot(p.reshape(hh * tq, ncp).astype(_BF16), vc_ref[...])
    ratio = L_SLC // STRIDE_CMP
    n_slc = ncp // ratio
    pool = (lax.broadcasted_iota(jnp.int32, (ncp, n_slc), 0) // ratio
            == lax.broadcasted_iota(jnp.int32, (ncp, n_slc), 1)).astype(_BF16)
    imp = _dot_exact01(jnp.sum(p, axis=0), pool)

    blk = lax.broadcasted_iota(jnp.int32, (tq, n_slc), 1)
    cur = tpos // L_SLC
    forced = (blk == 0) | (blk == cur) | (blk == cur - 1)
    future = blk * L_SLC > tpos
    score = jnp.where(future, -SEL_BONUS, imp + jnp.where(forced, SEL_BONUS, 0.0))
    st_ref[...] = score.T
    st = st_ref[...]
    b_idx = lax.broadcasted_iota(jnp.int32, (n_slc, tq), 0)

    def rank_body(bp, cnt):
        row = st_ref[pl.ds(bp, 1), :]
        ahead = (row > st) | ((row == st) & (b_idx > bp))
        return cnt + jnp.where(ahead, 1.0, 0.0)

    cnt = lax.fori_loop(0, n_slc, rank_body, jnp.zeros((n_slc, tq), _F32), unroll=8)
    sel_t = (cnt < float(N_SEL)) & (st > -0.5 * SEL_BONUS)
    sel = sel_t.astype(_F32).T.astype(_BF16)

    def slc_body(c, carry):
        m, l, acc = carry
        k0 = pl.multiple_of(c * tk, tk)
        kk = ks_ref[pl.ds(k0, tk), :]
        vv = vs_ref[pl.ds(k0, tk), :]
        sc = _dot_nt(q, kk).reshape(hh, tq, tk)
        kpos = k0 + lax.broadcasted_iota(jnp.int32, (1, tk), 1)
        expand = ((k0 + lax.broadcasted_iota(jnp.int32, (n_slc, tk), 1)) // L_SLC
                  == lax.broadcasted_iota(jnp.int32, (n_slc, tk), 0)).astype(_BF16)
        mk = (_dot(sel, expand) > 0.5) & (kpos <= tpos)
        mk = mk[None]
        sc = jnp.where(mk, sc, NEG_INF)
        m_new = jnp.maximum(m, jnp.max(sc, axis=-1, keepdims=True))
        alpha = jnp.exp(m - m_new)
        pp = jnp.exp(sc - m_new)
        l = alpha * l + jnp.sum(pp, axis=-1, keepdims=True)
        acc = acc * alpha.reshape(hh * tq, 1) + _dot(pp.reshape(hh * tq, tk).astype(_BF16), vv)
        return m_new, l, acc

    n_chunks = (q0 + tq + tk - 1) // tk
    m0 = jnp.full((hh, tq, 1), NEG_INF, _F32)
    l0 = jnp.zeros((hh, tq, 1), _F32)
    a0 = jnp.zeros((hh * tq, DH_NSA), _F32)
    _, l_s, acc_s = lax.fori_loop(0, n_chunks, slc_body, (m0, l0, a0))
    o_slc = acc_s / jnp.where(l_s > 0.0, l_s, 1.0).reshape(hh * tq, 1)

    wlen = WINDOW + tq
    w0 = pl.multiple_of(jnp.maximum(q0 - WINDOW, 0), tq)
    sw = _dot_nt(q, kw_ref[pl.ds(w0, wlen), :]).reshape(hh, tq, wlen)
    dist = tpos - (w0 + lax.broadcasted_iota(jnp.int32, (1, wlen), 1))
    wmask = (dist >= 0) & (dist <= WINDOW)
    pw = _softmax_rows(sw, wmask[None])
    o_win = _dot(pw.reshape(hh * tq, wlen).astype(_BF16), vw_ref[pl.ds(w0, wlen), :])

    g = g_ref[...]
    outs = []
    for h in range(hh):
        rows = slice(h * tq, (h + 1) * tq)
        outs.append(g[:, 3 * h:3 * h + 1] * o_cmp[rows] + g[:, 3 * h + 1:3 * h + 2] * o_slc[rows]
                    + g[:, 3 * h + 2:3 * h + 3] * o_win[rows])
    o_ref[...] = jnp.concatenate(outs, axis=1).astype(o_ref.dtype)


def nsa_prompt_pallas(q, kc, vc, ks_g, vs_g, kw_g, vw_g, gates, nc):
    t = q.shape[0]
    ncp = kc.shape[1]
    tq = _NSA_TQ
    assert t % _NSA_TK == 0 and t >= WINDOW + tq and ncp * STRIDE_CMP == t
    per_group = lambda a: a.astype(_BF16)
    ks_rows, vs_rows, kw_rows, vw_rows = ks_g, vs_g, kw_g, vw_g
    g3 = gates.reshape(t, G_NSA, HPG_NSA * 3).transpose(1, 0, 2)
    hw = HPG_NSA * DH_NSA
    kv_spec = pl.BlockSpec((None, t, DH_NSA), lambda g, i: (g, 0, 0))
    c_spec = pl.BlockSpec((None, ncp, DH_NSA), lambda g, i: (g, 0, 0))
    return pl.pallas_call(
        functools.partial(_nsa_prompt_kernel, nc=nc, t_total=t),
        grid=(G_NSA, t // tq),
        in_specs=[pl.BlockSpec((tq, hw), lambda g, i: (i, g)),
                  c_spec, c_spec, kv_spec, kv_spec, kv_spec, kv_spec,
                  pl.BlockSpec((None, tq, HPG_NSA * 3), lambda g, i: (g, i, 0))],
        out_specs=pl.BlockSpec((tq, hw), lambda g, i: (i, g)),
        out_shape=jax.ShapeDtypeStruct((t, H_NSA * DH_NSA), _BF16),
        scratch_shapes=[pltpu.VMEM((ncp // (L_SLC // STRIDE_CMP), tq), _F32)],
        compiler_params=pltpu.CompilerParams(
            dimension_semantics=("arbitrary", "arbitrary"), vmem_limit_bytes=_VMEM_LIMIT),
        name="nsa_prompt",
    )(q, kc.astype(_BF16), vc.astype(_BF16), per_group(ks_rows), per_group(vs_rows),
      per_group(kw_rows), per_group(vw_rows), g3)


_DIFF_TQ = 256
_DIFF_TK = 1024


def _diff_prompt_kernel(lam_ref, q_ref, k_ref, v_ref, gam_ref, o_ref, *, out_scale):
    tq, tk = _DIFF_TQ, _DIFF_TK
    q0 = pl.program_id(1) * tq
    qt = (q_ref[...] * (DH_DIFF ** -0.5)).astype(_BF16)
    q1, q2 = qt[:, :DH_DIFF], qt[:, DH_DIFF:]
    tpos = q0 + lax.broadcasted_iota(jnp.int32, (tq, 1), 0)

    def make_body(masked):
        def body(c, carry):
            m1, l1, a1, m2, l2, a2 = carry
            k0 = pl.multiple_of(c * tk, tk)
            kk = k_ref[pl.ds(k0, tk), :]
            vv = v_ref[pl.ds(k0, tk), :]
            if masked:
                mk = (k0 + lax.broadcasted_iota(jnp.int32, (1, tk), 1)) <= tpos

            def stream(qh, kh, m, l, a):
                sc = _dot_nt(qh, kh)
                if masked:
                    sc = jnp.where(mk, sc, NEG_INF)
                m_new = jnp.maximum(m, jnp.max(sc, axis=-1, keepdims=True))
                alpha = jnp.exp(m - m_new)
                pp = jnp.exp(sc - m_new)
                l = alpha * l + jnp.sum(pp, axis=-1, keepdims=True)
                a = alpha * a + _dot(pp.astype(_BF16), vv)
                return m_new, l, a

            m1, l1, a1 = stream(q1, kk[:, :DH_DIFF], m1, l1, a1)
            m2, l2, a2 = stream(q2, kk[:, DH_DIFF:], m2, l2, a2)
            return m1, l1, a1, m2, l2, a2
        return body

    n_chunks = (q0 + tq + tk - 1) // tk
    mi = jnp.full((tq, 1), NEG_INF, _F32)
    li = jnp.zeros((tq, 1), _F32)
    ai = jnp.zeros((tq, DV_DIFF), _F32)
    carry = lax.fori_loop(0, n_chunks - 1, make_body(False), (mi, li, ai, mi, li, ai))
    _, l1, a1, _, l2, a2 = make_body(True)(n_chunks - 1, carry)
    lam = lam_ref[0:1, 0:1]
    o = a1 / l1 - lam * (a2 / l2)
    o = o * lax.rsqrt(jnp.mean(o * o, axis=-1, keepdims=True) + LN_EPS) * gam_ref[...]
    o_ref[...] = (o * out_scale).astype(o_ref.dtype)


def diff_prompt_pallas(q, k, v, lam, subln_g, lam_init):
    t = q.shape[0]
    tq = _DIFF_TQ
    assert t % _DIFF_TK == 0 and _DIFF_TK % _DIFF_TQ == 0
    lam_row = jnp.full((8, 128), lam, _F32)
    hw = 2 * DH_DIFF
    return pl.pallas_call(
        functools.partial(_diff_prompt_kernel, out_scale=1.0 - lam_init),
        grid=(H_DIFF, t // tq),
        in_specs=[pl.BlockSpec((8, 128), lambda h, i: (0, 0)),
                  pl.BlockSpec((tq, hw), lambda h, i: (i, h)),
                  pl.BlockSpec((t, hw), lambda h, i: (0, h)),
                  pl.BlockSpec((t, DV_DIFF), lambda h, i: (0, h)),
                  pl.BlockSpec((1, DV_DIFF), lambda h, i: (0, 0))],
        out_specs=pl.BlockSpec((tq, DV_DIFF), lambda h, i: (i, h)),
        out_shape=jax.ShapeDtypeStruct((t, H_DIFF * DV_DIFF), _BF16),
        compiler_params=pltpu.CompilerParams(
            dimension_semantics=("arbitrary", "arbitrary"), vmem_limit_bytes=_VMEM_LIMIT),
        name="diff_prompt",
    )(lam_row, q, k.astype(_BF16), v.astype(_BF16), subln_g.reshape(1, DV_DIFF))


_SD_PAGES = 8
_SD_STREAMS = 2 * H_DIFF


def _diff_sample_kernel(pt_ref, lam_ref, q_ref, kn_ref, vn_ref, gam_ref, *refs, out_scale):
    k_refs, v_refs = refs[:_SD_PAGES], refs[_SD_PAGES:2 * _SD_PAGES]
    o_ref, m_ref, l_ref, acc_ref = refs[2 * _SD_PAGES:]
    c = pl.program_id(1)
    q = q_ref[...]
    own_head = lambda n: (lax.broadcasted_iota(jnp.int32, (_SD_STREAMS, n), 1) % H_DIFF
                          == lax.broadcasted_iota(jnp.int32, (_SD_STREAMS, n), 0) // 2)

    @pl.when(c == 0)
    def _():
        s0 = _dot_nt(q, kn_ref[...].astype(_BF16))
        m_ref[...] = jnp.max(jnp.where(own_head(H_DIFF), s0, NEG_INF), axis=-1, keepdims=True)
        l_ref[...] = jnp.ones(l_ref.shape, _F32)
        acc_ref[...] = vn_ref[...].astype(_BF16).astype(_F32)

    rows = PAGE_SIZE * H_DIFF
    flat = lambda r: r[...].reshape(rows, 2 * DH_DIFF).astype(_BF16)
    kk = jnp.concatenate([flat(r) for r in k_refs], axis=0)
    vv = jnp.concatenate([flat(r) for r in v_refs], axis=0)
    mk = own_head(_SD_PAGES * rows)
    sc = jnp.where(mk, _dot_nt(q, kk), NEG_INF)
    m_old = m_ref[...]
    m_new = jnp.maximum(m_old, jnp.max(sc, axis=-1, keepdims=True))
    alpha = jnp.exp(m_old - m_new)
    pp = jnp.where(mk, jnp.exp(sc - m_new), 0.0)
    m_ref[...] = m_new
    l_ref[...] = alpha * l_ref[...] + jnp.sum(pp, axis=-1, keepdims=True)
    acc_ref[...] = alpha * acc_ref[...] + _dot(pp.astype(_BF16), vv)

    @pl.when(c == pl.num_programs(1) - 1)
    def _():
        lam = lam_ref[0:1, 0:1]
        on = acc_ref[...] / l_ref[...]
        outs = []
        for h in range(H_DIFF):
            o = on[2 * h:2 * h + 1, :] - lam * on[2 * h + 1:2 * h + 2, :]
            o = o * lax.rsqrt(jnp.mean(o * o, axis=-1, keepdims=True) + LN_EPS) * gam_ref[...]
            outs.append(o * out_scale)
        o_ref[...] = jnp.concatenate(outs, axis=1)


def diff_sample_pallas(q1, q2, k_new, v_new, cache_k, cache_v, page_table, lam, subln_g, lam_init):
    b, n_pages = page_table.shape
    hw = 2 * DH_DIFF
    assert n_pages % _SD_PAGES == 0 and DV_DIFF == hw
    qs = jnp.stack([q1, q2], axis=2) * (DH_DIFF ** -0.5)
    q16 = jnp.einsum('bhjd,ji->bhjid', qs, jnp.eye(2, dtype=_F32)).reshape(b, _SD_STREAMS, hw).astype(_BF16)
    vn16 = jnp.repeat(v_new, 2, axis=1)
    lam_row = jnp.full((8, 128), lam, _F32)
    page_spec = lambda j: pl.BlockSpec((None, None, PAGE_SIZE, H_DIFF, hw),
                                       lambda bb, c, pt: (0, pt[bb, c * _SD_PAGES + j], 0, 0, 0))
    per_b = lambda rows: pl.BlockSpec((None, rows, hw), lambda bb, c, pt: (bb, 0, 0))
    grid_spec = pltpu.PrefetchScalarGridSpec(
        num_scalar_prefetch=1,
        grid=(b, n_pages // _SD_PAGES),
        in_specs=[pl.BlockSpec((8, 128), lambda bb, c, pt: (0, 0)),
                  per_b(_SD_STREAMS), per_b(H_DIFF), per_b(_SD_STREAMS),
                  pl.BlockSpec((1, DV_DIFF), lambda bb, c, pt: (0, 0))]
                 + [page_spec(j) for j in range(_SD_PAGES)] * 2,
        out_specs=pl.BlockSpec((None, 1, H_DIFF * DV_DIFF), lambda bb, c, pt: (bb, 0, 0)),
        scratch_shapes=[pltpu.VMEM((_SD_STREAMS, 1), _F32), pltpu.VMEM((_SD_STREAMS, 1), _F32),
                        pltpu.VMEM((_SD_STREAMS, DV_DIFF), _F32)],
    )
    out = pl.pallas_call(
        functools.partial(_diff_sample_kernel, out_scale=1.0 - lam_init),
        grid_spec=grid_spec,
        out_shape=jax.ShapeDtypeStruct((b, 1, H_DIFF * DV_DIFF), _F32),
        compiler_params=pltpu.CompilerParams(
            dimension_semantics=("arbitrary", "arbitrary"), vmem_limit_bytes=_VMEM_LIMIT),
        name="diff_sample",
    )(page_table, lam_row, q16, k_new, vn16, subln_g.reshape(1, DV_DIFF),
      *([cache_k] * _SD_PAGES), *([cache_v] * _SD_PAGES))
    return out.reshape(b, H_DIFF * DV_DIFF)


_PEER_PAIRS = tuple((x, y) for x in range(PEER_TOPK) for y in range(PEER_TOPK)
                    if (x + 1) * (y + 1) <= PEER_TOPK)
_PEER_NPAIR_PAD = -(-len(_PEER_PAIRS) // 8) * 8
_PEER_HALF = PEER_DQ // 2


def _distinct_top(x):
    vals, cnts = [], []
    for _ in range(PEER_TOPK):
        m = jnp.max(x, axis=0, keepdims=True)
        eq = x == m
        vals.append(m)
        cnts.append(jnp.sum(jnp.where(eq, 1.0, 0.0), axis=0, keepdims=True))
        x = jnp.where(eq, NEG_INF, x)
    return vals, cnts


def _peer_a_kernel(h_ref, wq_ref, sk_ref, s1_ref, e1_ref, s2_ref, e2_ref, tau_ref, c_ref, w_ref):
    tn = h_ref.shape[0]
    q = _dot(h_ref[...], wq_ref[...]).astype(_BF16)
    cols = lambda hh, p: q[:, (2 * hh + p) * _PEER_HALF:(2 * hh + p + 1) * _PEER_HALF]
    s1 = jnp.concatenate([_dot_nt(sk_ref[2 * hh], cols(hh, 0)) for hh in range(PEER_HEADS)], axis=1)
    s2 = jnp.concatenate([_dot_nt(sk_ref[2 * hh + 1], cols(hh, 1)) for hh in range(PEER_HEADS)], axis=1)
    a, ca = _distinct_top(s1)
    b, cb = _distinct_top(s2)
    c_ref[...] = jnp.full(c_ref.shape, NEG_INF, _F32)
    w_ref[...] = jnp.zeros(w_ref.shape, _F32)
    for k, (x, y) in enumerate(_PEER_PAIRS):
        c_ref[k:k + 1, :] = a[x] + b[y]
        w_ref[k:k + 1, :] = ca[x] * cb[y]
    cand, wgt = c_ref[...], w_ref[...]
    tau = jnp.full(a[0].shape, NEG_INF, _F32)
    for k in range(len(_PEER_PAIRS)):
        cu = c_ref[k:k + 1, :]
        n_ge = jnp.sum(jnp.where(cand >= cu, wgt, 0.0), axis=0, keepdims=True)
        tau = jnp.maximum(tau, jnp.where(n_ge >= float(PEER_TOPK), cu, NEG_INF))
    top = a[0] + b[0]
    z = jnp.sum(jnp.where(cand >= tau, wgt * jnp.exp(cand - top), 0.0), axis=0, keepdims=True)
    e1 = jnp.exp(s1 - a[0])
    e2 = jnp.exp(s2 - b[0]) / z
    for hh in range(PEER_HEADS):
        lanes = slice(hh * tn, (hh + 1) * tn)
        s1_ref[hh] = s1[:, lanes]
        e1_ref[hh] = e1[:, lanes]
        s2_ref[hh] = s2[:, lanes]
        e2_ref[hh] = e2[:, lanes]
        tau_ref[hh:hh + 1, :] = tau[:, lanes]


_PEER_RB = 32
_PEER_PIECE = 256


def _peer_b_kernel(h_ref, s1_ref, e1_ref, s2_ref, e2_ref, tau_ref, u_ref, v_ref, o_ref, g_ref):
    tn = h_ref.shape[0]
    c = pl.program_id(1)
    n_sub = u_ref.shape[0] // N_KEYS
    s1_rows = [[s1_ref[hh, pl.ds(c * n_sub + ii, 1), :] for ii in range(n_sub)] for hh in range(PEER_HEADS)]
    e1_rows = [[e1_ref[hh, pl.ds(c * n_sub + ii, 1), :] for ii in range(n_sub)] for hh in range(PEER_HEADS)]
    w_parts = []
    pw = min(_PEER_PIECE, tn)
    for piece in range(tn // pw):
        tok = slice(piece * pw, (piece + 1) * pw)
        a_t = _dot_nt(u_ref[...], h_ref[tok, :])
        for cb in range(piece * pw // 128, (piece + 1) * pw // 128):
            lanes = slice(cb * 128, (cb + 1) * 128)
            for rb in range(N_KEYS // _PEER_RB):
                rows = slice(rb * _PEER_RB, (rb + 1) * _PEER_RB)
                accs = [jnp.zeros((_PEER_RB, 128), _F32) for _ in range(n_sub)]
                for hh in range(PEER_HEADS):
                    s2 = s2_ref[hh, rows, lanes]
                    e2 = e2_ref[hh, rows, lanes]
                    tau = tau_ref[hh:hh + 1, lanes]
                    for ii in range(n_sub):
                        pair = s2 + s1_rows[hh][ii][:, lanes]
                        accs[ii] = accs[ii] + jnp.where(pair >= tau, e2, 0.0) * e1_rows[hh][ii][:, lanes]
                for ii in range(n_sub):
                    g_ref[ii * N_KEYS + rb * _PEER_RB:ii * N_KEYS + (rb + 1) * _PEER_RB, lanes] = accs[ii]
        w_parts.append((g_ref[:, tok] * _gelu_erf(a_t)).astype(_BF16))
    w_t = jnp.concatenate(w_parts, axis=1)
    contrib = lax.dot_general(w_t, v_ref[...], (((0,), (0,)), ((), ())), preferred_element_type=_F32)

    @pl.when(c == 0)
    def _():
        o_ref[...] = contrib

    @pl.when(c > 0)
    def _():
        o_ref[...] += contrib


_PEER_E_CHUNK = 1024
_PEER_TN = 512


def peer_ffn_pallas(h, wq, subkeys, u, v, *, tn):
    n, d = h.shape
    assert n % tn == 0
    hb = h.astype(_BF16)
    r = PEER_HEADS * tn
    stat = jax.ShapeDtypeStruct((PEER_HEADS, N_KEYS, n), _F32)
    stat_spec = pl.BlockSpec((PEER_HEADS, N_KEYS, tn), lambda i: (0, 0, i))
    s1, e1, s2, e2, tau = pl.pallas_call(
        _peer_a_kernel,
        grid=(n // tn,),
        in_specs=[pl.BlockSpec((tn, d), lambda i: (i, 0)),
                  pl.BlockSpec(wq.shape, lambda i: (0, 0)),
                  pl.BlockSpec(subkeys.shape, lambda i: (0, 0, 0))],
        out_specs=[stat_spec, stat_spec, stat_spec, stat_spec,
                   pl.BlockSpec((PEER_HEADS, tn), lambda i: (0, i))],
        out_shape=[stat, stat, stat, stat, jax.ShapeDtypeStruct((PEER_HEADS, n), _F32)],
        scratch_shapes=[pltpu.VMEM((_PEER_NPAIR_PAD, r), _F32), pltpu.VMEM((_PEER_NPAIR_PAD, r), _F32)],
        compiler_params=pltpu.CompilerParams(
            dimension_semantics=("arbitrary",), vmem_limit_bytes=_VMEM_LIMIT),
        name="peer_scores",
    )(hb, wq, subkeys)
    e = _PEER_E_CHUNK
    stat_spec_b = pl.BlockSpec((PEER_HEADS, N_KEYS, tn), lambda i, c: (0, 0, i))
    return pl.pallas_call(
        _peer_b_kernel,
        grid=(n // tn, N_EXPERTS // e),
        in_specs=[pl.BlockSpec((tn, d), lambda i, c: (i, 0)),
                  stat_spec_b, stat_spec_b, stat_spec_b, stat_spec_b,
                  pl.BlockSpec((PEER_HEADS, tn), lambda i, c: (0, i)),
                  pl.BlockSpec((e, d), lambda i, c: (c, 0)),
                  pl.BlockSpec((e, d), lambda i, c: (c, 0))],
        out_specs=pl.BlockSpec((tn, d), lambda i, c: (i, 0)),
        out_shape=jax.ShapeDtypeStruct((n, d), _F32),
        scratch_shapes=[pltpu.VMEM((e, tn), _F32)],
        compiler_params=pltpu.CompilerParams(
            dimension_semantics=("arbitrary", "arbitrary"), vmem_limit_bytes=_VMEM_LIMIT),
        name="peer_experts",
    )(hb, s1, e1, s2, e2, tau, u, v)


def _ln_residual_kernel(a_ref, f_ref, g_ref, b_ref, o_ref):
    x = ALPHA * a_ref[...] + f_ref[...]
    mu = jnp.mean(x, axis=-1, keepdims=True)
    xc = x - mu
    var = jnp.mean(xc * xc, axis=-1, keepdims=True)
    o_ref[...] = xc * lax.rsqrt(var + LN_EPS) * g_ref[...] + b_ref[...]


def ln_residual(a, f, g, b, *, tm=256):
    n, d = a.shape
    tm = min(tm, n)
    return pl.pallas_call(
        _ln_residual_kernel,
        grid=(n // tm,),
        in_specs=[pl.BlockSpec((tm, d), lambda i: (i, 0)),
                  pl.BlockSpec((tm, d), lambda i: (i, 0)),
                  pl.BlockSpec((1, d), lambda i: (0, 0)),
                  pl.BlockSpec((1, d), lambda i: (0, 0))],
        out_specs=pl.BlockSpec((tm, d), lambda i: (i, 0)),
        out_shape=jax.ShapeDtypeStruct((n, d), jnp.float32),
        name="ln_residual",
    )(a, f, g.reshape(1, d), b.reshape(1, d))


def prompt_attention(pj, cmp_k, cmp_v, lam, subln_g, lam_init):
    t = pj["q_n"].shape[0]
    nc = (t - L_CMP) // STRIDE_CMP + 1
    kc_c = compress_pallas(pj["kf"][0], *cmp_k)
    vc_c = compress_pallas(pj["vf"][0], *cmp_v)
    kb, vb = pj["kb"], pj["vb"]
    o_n = nsa_prompt_pallas(pj["q_n"], kc_c, vc_c, kb[1], vb[1], kb[2], vb[2], pj["g_n"], nc)
    o_d = diff_prompt_pallas(pj["q_d"], pj["k_db"], pj["v_db"], lam, subln_g, lam_init)
    return o_n, o_d


def peer_block(h, ln2_g, ln2_b, peer_w):
    n = h.shape[0]
    tn = min(_PEER_TN, -(-n // Q_BLOCK) * Q_BLOCK)
    n_pad = -(-n // tn) * tn
    f = peer_ffn_pallas(jnp.pad(h, ((0, n_pad - n), (0, 0))), *peer_w, tn=tn)[:n]
    return ln_residual(h, f, ln2_g, ln2_b)


def post_block(x, mix, ln1_g, ln1_b, ln2_g, ln2_b, peer_w):
    b, t, d = x.shape
    h = ln_residual(x.reshape(b * t, d), mix.reshape(b * t, d), ln1_g, ln1_b)
    return peer_block(h, ln2_g, ln2_b, peer_w).reshape(b, t, d)


def kernel(x_prompt, x_sample, cache_diff_k, cache_diff_v, cache_nsa_cmp_k, cache_nsa_cmp_v,
           cache_nsa_slc_k, cache_nsa_slc_v, state_nsa_win_k, state_nsa_win_v, page_table,
           w_in, cmp_w1_k, cmp_pe_k, cmp_w2_k, cmp_w1_v, cmp_pe_v, cmp_w2_v,
           lambda_q1, lambda_k1, lambda_q2, lambda_k2, diff_subln_g,
           w_br_nsa, w_br_diff, w_out, ln1_g, ln1_b,
           peer_wq, peer_subkeys, peer_u, peer_v, ln2_g, ln2_b):
    past_len = page_table.shape[1] * PAGE_SIZE
    seq, dec_seq = x_prompt.shape[1], x_sample.shape[1]
    assert x_prompt.shape[0] == 1 and dec_seq == 1 and w_in.shape[0] == 1
    pos_p = jnp.arange(seq, dtype=jnp.int32)
    pos_s = past_len + jnp.arange(dec_seq, dtype=jnp.int32)
    l = 0
    lam_init = 0.8 - 0.6 * math.exp(-0.3 * l)
    lam = (jnp.exp(jnp.sum(lambda_q1[l] * lambda_k1[l]))
           - jnp.exp(jnp.sum(lambda_q2[l] * lambda_k2[l])) + lam_init)
    cmp_k = (cmp_w1_k[l], cmp_pe_k[l], cmp_w2_k[l])
    cmp_v = (cmp_w1_v[l], cmp_pe_v[l], cmp_w2_v[l])
    peer_w = (peer_wq[l].astype(_BF16),
              peer_subkeys[l].reshape(PEER_HEADS * 2, N_KEYS, _PEER_HALF).astype(_BF16),
              peer_u[l].astype(_BF16), peer_v[l].astype(_BF16))

    wb = regroup_w_in(w_in[l])
    out_w = (w_br_nsa[l].astype(_BF16), w_br_diff[l].astype(_BF16), w_out[l].astype(_BF16))
    d = D_MODEL

    def kv_outputs(pj, nb, t):
        rows = lambda a: a.reshape(nb, t, G_NSA, DH_NSA)
        kf, vf = pj["kf"], pj["vf"]
        return (pj["k_d"].reshape(nb, t, H_DIFF, 2 * DH_DIFF), pj["v_d"].reshape(nb, t, H_DIFF, DV_DIFF),
                rows(kf[0]), rows(vf[0]), rows(kf[1]), rows(vf[1]), rows(kf[2]), rows(vf[2]))

    xp2 = x_prompt.reshape(seq, d)
    pj = mixer_project_pallas(xp2, pos_p, wb)
    o_n, o_d = prompt_attention(pj, cmp_k, cmp_v, lam, diff_subln_g[l], lam_init)
    hp = mixer_output_ln_pallas(o_n, o_d, pj["g_m"], xp2, *out_w, ln1_g[l], ln1_b[l])
    xp = peer_block(hp, ln2_g[l], ln2_b[l], peer_w).reshape(1, seq, d)
    kd, vd, kc, vc, ks, vs, kw, vw = kv_outputs(pj, 1, seq)
    keep = min(WINDOW, seq)
    outs_p = (kd, vd, kc, vc, ks, vs, kw[:, seq - keep:], vw[:, seq - keep:])

    nb = x_sample.shape[0]
    xs2 = x_sample.reshape(nb, d)
    pj = mixer_project_pallas(xs2, jnp.broadcast_to(pos_s, (nb,)), wb)
    kd, vd, kc, vc, ks, vs, kw, vw = kv_outputs(pj, nb, 1)
    q_n = pj["q_n"].astype(_F32).reshape(nb, 1, G_NSA, HPG_NSA, DH_NSA)
    g_n = pj["g_n"].reshape(nb, 1, H_NSA, 3)
    t_total = past_len + dec_seq
    kc_c = compress_paged_pallas(cache_nsa_cmp_k, page_table, *cmp_k, t_total)
    vc_c = compress_paged_pallas(cache_nsa_cmp_v, page_table, *cmp_v, t_total)
    o_n, win_k_new, win_v_new = nsa_sample(q_n, kc_c, vc_c, ks, vs, kw, vw, g_n,
                                           cache_nsa_slc_k, cache_nsa_slc_v,
                                           state_nsa_win_k[l], state_nsa_win_v[l], page_table, l)
    q_d = pj["q_d"].astype(_F32).reshape(nb, H_DIFF, 2, DH_DIFF)
    o_d = diff_sample_pallas(q_d[:, :, 0], q_d[:, :, 1], kd[:, 0], vd[:, 0], cache_diff_k, cache_diff_v,
                             page_table, lam, diff_subln_g[l], lam_init)
    hs = mixer_output_ln_pallas(o_n.reshape(nb, -1).astype(_BF16), o_d.astype(_BF16), pj["g_m"], xs2,
                                *out_w, ln1_g[l], ln1_b[l])
    xs = peer_block(hs, ln2_g[l], ln2_b[l], peer_w).reshape(nb, 1, d)
    outs_s = (kd, vd, kc, vc, ks, vs, win_k_new, win_v_new)

    return (xp, xs) + tuple(a[None] for a in outs_p) + tuple(a[None] for a in outs_s)
```
